```python
import jax
import jax.numpy as jnp
from jax import lax
import numpy as np

D_MODEL = 1024
BATCH = 16
SEQ = 256
DEPTH = 4
DEC_BATCH = 4
DEC_SEQ = 2048
PAST_LEN = 256

GRID_W = 64
N_MIXERS = 3
N_POOL = len(range(0, DEPTH, N_MIXERS))
N_MLA = len(range(1, DEPTH, N_MIXERS))
N_NA = len(range(2, DEPTH, N_MIXERS))

POOL_WINDOWS = (2, 4, 8, 16)
POOL_GROUPS = len(POOL_WINDOWS)
POOL_GROUP_DIM = D_MODEL // POOL_GROUPS

MLA_HEADS = 8
QK_NOPE = 128
QK_ROPE = 64
V_HEAD = 128
Q_LORA = D_MODEL // 2
KV_LORA = D_MODEL // 4
MLA_WIDTH = MLA_HEADS * V_HEAD
MLA_SCALE = (QK_NOPE + QK_ROPE) ** -0.5

NA_HEADS = 16
NA_HEAD_DIM = 64
NA_WIDTH = NA_HEADS * NA_HEAD_DIM
NA_WIN_R = 8
NA_WIN_C = 16
NA_Q_COLS = 16
NA_K_COLS = 2 * NA_WIN_C
NA_SCALE = NA_HEAD_DIM ** -0.5

Q_BLOCK = 128
ROPE_BASE = 10000.0
LN_EPS = 1e-5
RMS_EPS = 1e-6
NEG_INF = -1e30
DEEPNORM_ALPHA = (2 * DEPTH) ** 0.25
DEEPNORM_BETA = (8 * DEPTH) ** -0.25

kernel_name = 'hybrid_diffusion_pool_mla_natten_step'


def layer_norm(x, g, b):
    xf = x.astype(jnp.float32)
    mu = jnp.mean(xf, -1, keepdims=True)
    var = jnp.mean(jnp.square(xf - mu), -1, keepdims=True)
    return ((xf - mu) * lax.rsqrt(var + LN_EPS) * g + b).astype(x.dtype)


def rms_norm(x, g):
    xf = x.astype(jnp.float32)
    return (xf * lax.rsqrt(jnp.mean(xf * xf, -1, keepdims=True) + RMS_EPS) * g).astype(x.dtype)


def adaln(cond, w, b):
    m = jax.nn.silu(cond) @ w + b
    return jnp.split(m[:, None, :], 3, axis=-1)


def axial_rope(x):
    L, dr = x.shape[1], x.shape[-1]
    nf = dr // 4
    inv = ROPE_BASE ** (-jnp.arange(nf, dtype=jnp.float32) / nf)
    t = jnp.arange(L)
    pos = jnp.stack([t // GRID_W, t % GRID_W], -1).astype(jnp.float32)
    ang = pos[:, :, None] * inv
    shape = (1, L) + (1,) * (x.ndim - 3) + (2, nf)
    cos = jnp.cos(ang).reshape(shape)
    sin = jnp.sin(ang).reshape(shape)
    xs = x.astype(jnp.float32).reshape(x.shape[:-1] + (2, 2 * nf))
    x1, x2 = jnp.split(xs, 2, axis=-1)
    out = jnp.concatenate([x1 * cos - x2 * sin, x1 * sin + x2 * cos], -1)
    return out.reshape(x.shape).astype(x.dtype)


def blocked_attention(q, k, v, scale):
    B, Lq, H, dk = q.shape
    nb = Lq // Q_BLOCK
    qb = q.reshape(B, nb, Q_BLOCK, H, dk).transpose(1, 0, 2, 3, 4)

    def one_block(qblk):
        s = jnp.einsum('bqhd,bkhd->bhqk', qblk, k).astype(jnp.float32) * scale
        p = jax.nn.softmax(s, axis=-1)
        return jnp.einsum('bhqk,bkhd->bqhd', p.astype(v.dtype), v)

    o = lax.map(one_block, qb)
    return o.transpose(1, 0, 2, 3, 4).reshape(B, Lq, H, v.shape[-1])


def pool_branch(h, w_in, w_grp, scale, w_out):
    u, z = jnp.split(h @ w_in, 2, axis=-1)
    B, L, _ = u.shape
    ug = u.reshape(B, L, POOL_GROUPS, POOL_GROUP_DIM).astype(jnp.float32)
    csum = jnp.concatenate([jnp.zeros_like(ug[:, :1]), jnp.cumsum(ug, axis=1)], axis=1)
    t = jnp.arange(L)[:, None]
    w = jnp.array(POOL_WINDOWS)[None, :]
    lo = jnp.clip(t - w // 2, 0, L)
    hi = jnp.clip(t + w - w // 2, 0, L)
    gi = jnp.arange(POOL_GROUPS)[None, :]
    win_sum = csum[:, hi, gi] - csum[:, lo, gi]
    cnt = (hi - lo).astype(jnp.float32)[None, :, :, None]
    mixed = (win_sum / cnt - ug).astype(h.dtype)
    mixed = jnp.einsum('blgc,gcd->blgd', mixed, w_grp).reshape(B, L, D_MODEL) * scale
    return (mixed * jax.nn.silu(z)) @ w_out


def mla_inputs(h, w_in, q_norm, w_uq, kv_norm):
    B, L, _ = h.shape
    cq, ckv, kr, z = jnp.split(h @ w_in, [Q_LORA, Q_LORA + KV_LORA, Q_LORA + KV_LORA + QK_ROPE], axis=-1)
    q = (rms_norm(cq, q_norm) @ w_uq).reshape(B, L, MLA_HEADS, QK_NOPE + QK_ROPE)
    return q, rms_norm(ckv, kv_norm), kr, z


def mla_keys_values(ckv, kr, w_ukv):
    B, L, _ = ckv.shape
    kv = (ckv @ w_ukv).reshape(B, L, MLA_HEADS, QK_NOPE + V_HEAD)
    k_nope, v = jnp.split(kv, [QK_NOPE], axis=-1)
    k = jnp.concatenate([k_nope, jnp.broadcast_to(kr[:, :, None, :], (B, L, MLA_HEADS, QK_ROPE))], axis=-1)
    return k, v


def mla_context(h, w_in, q_norm, w_uq, kv_norm, w_ukv, w_out):
    B, L, _ = h.shape
    q, ckv, kr, z = mla_inputs(h, w_in, q_norm, w_uq, kv_norm)
    k, v = mla_keys_values(ckv, kr, w_ukv)
    o = blocked_attention(q, k, v, MLA_SCALE)
    y = (o.reshape(B, L, MLA_WIDTH) * jax.nn.silu(z)) @ w_out
    return y, ckv, kr


def mla_latent(h, ckv_ctx, kr_ctx, w_in, q_norm, w_uq, kv_norm, w_ukv, w_out):
    B, L, _ = h.shape
    q, ckv, kr, z = mla_inputs(h, w_in, q_norm, w_uq, kv_norm)
    q = jnp.concatenate([q[..., :QK_NOPE], axial_rope(q[..., QK_NOPE:])], axis=-1)
    k_ctx, v_ctx = mla_keys_values(ckv_ctx, kr_ctx, w_ukv)
    k_lat, v_lat = mla_keys_values(ckv, axial_rope(kr), w_ukv)
    k = jnp.concatenate([k_ctx, k_lat], axis=1)
    v = jnp.concatenate([v_ctx, v_lat], axis=1)
    o = blocked_attention(q, k, v, MLA_SCALE)
    return (o.reshape(B, L, MLA_WIDTH) * jax.nn.silu(z)) @ w_out


def na_inputs(h, w_in):
    B, L, _ = h.shape
    q, k, v, z = jnp.split(h @ w_in, 4, axis=-1)
    shp = (B, L, NA_HEADS, NA_HEAD_DIM)
    return q.reshape(shp), k.reshape(shp), v.reshape(shp), z


def na_context(h, w_in, w_out):
    B, L, _ = h.shape
    q, k, v, z = na_inputs(h, w_in)
    o = blocked_attention(q, k, v, NA_SCALE)
    y = (o.reshape(B, L, NA_WIDTH) * jax.nn.silu(z)) @ w_out
    return y, k, v


def neighbourhood_attention(q, k, v, k_ctx, v_ctx, rpb):
    B, L, H, dh = q.shape
    rows = L // GRID_W
    kr = min(NA_WIN_R, rows)
    ncb = GRID_W // NA_Q_COLS
    qcols = np.arange(GRID_W).reshape(ncb, NA_Q_COLS)
    cstart = np.clip(qcols - NA_WIN_C // 2, 0, GRID_W - NA_WIN_C)
    bstart = np.clip(np.arange(ncb) * NA_Q_COLS - NA_WIN_C // 2, 0, GRID_W - NA_K_COLS)
    kcols = bstart[:, None] + np.arange(NA_K_COLS)
    col_valid = (kcols[:, None, :] >= cstart[:, :, None]) & (kcols[:, None, :] < cstart[:, :, None] + NA_WIN_C)
    dc_idx = np.clip(kcols[:, None, :] - qcols[:, :, None] + NA_WIN_C - 1, 0, 2 * NA_WIN_C - 2)
    qg = q.reshape(B, rows, GRID_W, H, dh).transpose(1, 0, 2, 3, 4)
    kg = k.reshape(B, rows, GRID_W, H, dh)
    vg = v.reshape(B, rows, GRID_W, H, dh)
    n_loc = kr * NA_K_COLS

    def row_block(args):
        r, q_r = args
        rs = jnp.clip(r - kr // 2, 0, rows - kr)
        kb = lax.dynamic_slice_in_dim(kg, rs, kr, axis=1)[:, :, kcols]
        vb = lax.dynamic_slice_in_dim(vg, rs, kr, axis=1)[:, :, kcols]
        qb = q_r.reshape(B, ncb, NA_Q_COLS, H, dh)
        s_loc = jnp.einsum('bjqhd,bajkhd->bhjqak', qb, kb).astype(jnp.float32) * NA_SCALE
        dr_idx = rs + jnp.arange(kr) - r + NA_WIN_R - 1
        bias = rpb[:, dr_idx[:, None, None, None], dc_idx[None]]
        s_loc = s_loc + bias.transpose(0, 2, 3, 1, 4)[None].astype(jnp.float32)
        s_loc = jnp.where(col_valid[None, None, :, :, None, :], s_loc, NEG_INF)
        s_loc = s_loc.reshape(B, H, ncb, NA_Q_COLS, n_loc)
        s_ctx = jnp.einsum('bjqhd,bchd->bhjqc', qb, k_ctx).astype(jnp.float32) * NA_SCALE
        p = jax.nn.softmax(jnp.concatenate([s_loc, s_ctx], axis=-1), axis=-1).astype(v.dtype)
        p_loc = p[..., :n_loc].reshape(B, H, ncb, NA_Q_COLS, kr, NA_K_COLS)
        o = jnp.einsum('bhjqak,bajkhd->bjqhd', p_loc, vb) + jnp.einsum('bhjqc,bchd->bjqhd', p[..., n_loc:], v_ctx)
        return o.reshape(B, GRID_W, H, dh)

    o = lax.map(row_block, (jnp.arange(rows), qg))
    return o.transpose(1, 0, 2, 3, 4).reshape(B, L, H * dh)


def na_latent(h, k_ctx, v_ctx, w_in, rpb, w_out):
    q, k, v, z = na_inputs(h, w_in)
    o = neighbourhood_attention(q, k, v, k_ctx, v_ctx, rpb)
    return (o * jax.nn.silu(z)) @ w_out


def setup_inputs(seed: int = 0) -> dict:
    key = jax.random.key(seed)
    ks = jax.random.split(key, 25)
    f32 = jnp.float32
    nrm = lambda k, shape, s=1.0: jax.random.normal(k, shape, f32) * s
    D = D_MODEL
    return {
        'x_prompt': nrm(ks[0], (BATCH, SEQ, D)),
        'x_sample': nrm(ks[1], (DEC_BATCH, DEC_SEQ, D)),
        'cache_mla_ckv': nrm(ks[2], (DEC_BATCH, N_MLA, PAST_LEN, KV_LORA)),
        'cache_mla_krope': nrm(ks[3], (DEC_BATCH, N_MLA, PAST_LEN, QK_ROPE)),
        'cache_na_k': nrm(ks[4], (DEC_BATCH, N_NA, PAST_LEN, NA_HEADS, NA_HEAD_DIM)),
        'cache_na_v': nrm(ks[5], (DEC_BATCH, N_NA, PAST_LEN, NA_HEADS, NA_HEAD_DIM)),
        'c': nrm(ks[6], (DEC_BATCH, D)),
        'c_ctx': nrm(ks[7], (D,)),
        'ada_w': nrm(ks[8], (DEPTH, D, 3 * D), 0.5 * D ** -0.5),
        'ada_b': nrm(ks[9], (DEPTH, 3 * D), 0.02),
        'ln_g': 1.0 + nrm(ks[10], (DEPTH, D), 0.02),
        'ln_b': nrm(ks[11], (DEPTH, D), 0.02),
        'pool_w_in': nrm(ks[12], (N_POOL, D, 2 * D), D ** -0.5),
        'pool_w_grp': nrm(ks[13], (N_POOL, POOL_GROUPS, POOL_GROUP_DIM, POOL_GROUP_DIM), POOL_GROUP_DIM ** -0.5),
        'pool_scale': 1.0 + nrm(ks[14], (N_POOL, D), 0.02),
        'pool_w_out': nrm(ks[15], (N_POOL, D, D), DEEPNORM_BETA * D ** -0.5),
        'mla_w_in': nrm(ks[16], (N_MLA, D, Q_LORA + KV_LORA + QK_ROPE + MLA_WIDTH), D ** -0.5),
        'mla_q_norm': 1.0 + nrm(ks[17], (N_MLA, Q_LORA), 0.02),
        'mla_w_uq': nrm(ks[18], (N_MLA, Q_LORA, MLA_HEADS * (QK_NOPE + QK_ROPE)), Q_LORA ** -0.5),
        'mla_kv_norm': 1.0 + nrm(ks[19], (N_MLA, KV_LORA), 0.02),
        'mla_w_ukv': nrm(ks[20], (N_MLA, KV_LORA, MLA_HEADS * (QK_NOPE + V_HEAD)), KV_LORA ** -0.5),
        'mla_w_out': nrm(ks[21], (N_MLA, MLA_WIDTH, D), DEEPNORM_BETA * MLA_WIDTH ** -0.5),
        'na_w_in': nrm(ks[22], (N_NA, D, 4 * NA_WIDTH), D ** -0.5),
        'na_rpb': nrm(ks[23], (N_NA, NA_HEADS, 2 * NA_WIN_R - 1, 2 * NA_WIN_C - 1), 0.1),
        'na_w_out': nrm(ks[24], (N_NA, NA_WIDTH, D), DEEPNORM_BETA * NA_WIDTH ** -0.5),
    }


def reference(x_prompt, x_sample, cache_mla_ckv, cache_mla_krope, cache_na_k, cache_na_v, c, c_ctx,
              ada_w, ada_b, ln_g, ln_b, pool_w_in, pool_w_grp, pool_scale, pool_w_out,
              mla_w_in, mla_q_norm, mla_w_uq, mla_kv_norm, mla_w_ukv, mla_w_out,
              na_w_in, na_rpb, na_w_out):
    yp = x_prompt
    ys = x_sample
    st_ckv, st_kr, st_k, st_v = [], [], [], []
    for i in range(DEPTH):
        kind, j = i % N_MIXERS, i // N_MIXERS
        sh_p, sc_p, g_p = adaln(c_ctx[None], ada_w[i], ada_b[i])
        sh_s, sc_s, g_s = adaln(c, ada_w[i], ada_b[i])
        hp = yp * (1.0 + sc_p) + sh_p
        hs = ys * (1.0 + sc_s) + sh_s
        if kind == 0:
            op = pool_branch(hp, pool_w_in[j], pool_w_grp[j], pool_scale[j], pool_w_out[j])
            os_ = pool_branch(hs, pool_w_in[j], pool_w_grp[j], pool_scale[j], pool_w_out[j])
        elif kind == 1:
            op, ckv, kr = mla_context(hp, mla_w_in[j], mla_q_norm[j], mla_w_uq[j], mla_kv_norm[j], mla_w_ukv[j], mla_w_out[j])
            st_ckv.append(ckv)
            st_kr.append(kr)
            os_ = mla_latent(hs, cache_mla_ckv[:, j], cache_mla_krope[:, j], mla_w_in[j], mla_q_norm[j],
                             mla_w_uq[j], mla_kv_norm[j], mla_w_ukv[j], mla_w_out[j])
        else:
            op, kc, vc = na_context(hp, na_w_in[j], na_w_out[j])
            st_k.append(kc)
            st_v.append(vc)
            os_ = na_latent(hs, cache_na_k[:, j], cache_na_v[:, j], na_w_in[j], na_rpb[j], na_w_out[j])
        yp = layer_norm(DEEPNORM_ALPHA * yp + g_p * op, ln_g[i], ln_b[i])
        ys = layer_norm(DEEPNORM_ALPHA * ys + g_s * os_, ln_g[i], ln_b[i])
    state_mla_ckv = jnp.stack(st_ckv, axis=1)
    state_mla_krope = jnp.stack(st_kr, axis=1)
    state_na_k = jnp.stack(st_k, axis=1)
    state_na_v = jnp.stack(st_v, axis=1)
    return (yp, ys, state_mla_ckv, state_mla_krope, state_na_k, state_na_v)
```

```python
import functools
import math

import numpy as np
import jax
import jax.numpy as jnp
from jax import lax
from jax.experimental import pallas as pl
from jax.experimental.pallas import tpu as pltpu

F32 = jnp.float32
BF16 = jnp.bfloat16

D_MODEL = 1024
DEPTH = 4
GRID_W = 64
N_MIXERS = 3

POOL_WINDOWS = (2, 4, 8, 16)
POOL_GROUP_DIM = D_MODEL // len(POOL_WINDOWS)
POOL_HALO = 8

MLA_HEADS = 8
QK_NOPE = 128
QK_ROPE = 64
V_HEAD = 128
Q_LORA = D_MODEL // 2
KV_LORA = D_MODEL // 4
MLA_WIDTH = MLA_HEADS * V_HEAD
MLA_SCALE = (QK_NOPE + QK_ROPE) ** -0.5

NA_HEADS = 16
NA_HEAD_DIM = 64
NA_WIDTH = NA_HEADS * NA_HEAD_DIM
NA_WIN_R = 8
NA_WIN_C = 16
NA_SCALE = NA_HEAD_DIM ** -0.5
NA_PAIRS = NA_HEADS // 2

ROPE_BASE = 10000.0
LN_EPS = 1e-5
RMS_EPS = 1e-6
NEG_INF = -1e30
DEEPNORM_ALPHA = (2 * DEPTH) ** 0.25
LOG2E = math.log2(math.e)

LANES = 128
ROW_TILE = 256
VMEM_LIMIT = 56 * 1024 * 1024


def _params(n_axes):
    return pltpu.CompilerParams(dimension_semantics=("arbitrary",) * n_axes, vmem_limit_bytes=VMEM_LIMIT)


def _silu(x):
    return x * jax.nn.sigmoid(x)


def _dot(a, b):
    return jnp.dot(a, b, preferred_element_type=F32)


def _dot_nt(a, b):
    return lax.dot_general(a, b, (((1,), (1,)), ((), ())), preferred_element_type=F32)


def _modulation(mod_ref):
    m = mod_ref[0]
    return m[:, :D_MODEL], m[:, D_MODEL:2 * D_MODEL], m[:, 2 * D_MODEL:]


def _deepnorm_ln(x, gate, branch, g, b):
    xf = DEEPNORM_ALPHA * x + gate * branch
    mu = jnp.mean(xf, axis=-1, keepdims=True)
    xc = xf - mu
    var = jnp.mean(xc * xc, axis=-1, keepdims=True)
    return xc * lax.rsqrt(var + LN_EPS) * g + b


def _rms(x, g):
    return x * lax.rsqrt(jnp.mean(x * x, axis=-1, keepdims=True) + RMS_EPS) * g


def _ada_kernel(cond_ref, w_ref, b_ref, o_ref):
    c = cond_ref[...]
    o_ref[0] = _dot(_silu(c).astype(BF16), w_ref[0].astype(BF16)) + b_ref[0]


def _ada(cond, ada_w, ada_b):
    tn = D_MODEL
    return pl.pallas_call(
        _ada_kernel,
        grid=(DEPTH, 3 * D_MODEL // tn),
        in_specs=[
            pl.BlockSpec((8, D_MODEL), lambda l, n: (0, 0)),
            pl.BlockSpec((1, D_MODEL, tn), lambda l, n: (l, 0, n)),
            pl.BlockSpec((1, 1, tn), lambda l, n: (l, 0, n)),
        ],
        out_specs=pl.BlockSpec((1, 8, tn), lambda l, n: (l, 0, n)),
        out_shape=jax.ShapeDtypeStruct((DEPTH, 8, 3 * D_MODEL), F32),
        compiler_params=_params(2),
        name="ada",
    )(cond, ada_w, ada_b.reshape(DEPTH, 1, 3 * D_MODEL))


def _pool_kernel(seq_len, x_ref, xp_ref, xn_ref, mod_ref, wu_ref, wz_ref, wg_ref, ps_ref, wo_ref,
                 lng_ref, lnb_ref, o_ref, useq_ref):
    t_rows = x_ref.shape[1]
    i = pl.program_id(1)
    n_tiles = pl.num_programs(1)
    shift, scale, gate = _modulation(mod_ref)
    one_scale = 1.0 + scale
    x = x_ref[0]
    hm = x * one_scale + shift
    hp = xp_ref[0] * one_scale + shift
    hn = xn_ref[0] * one_scale + shift
    hext = jnp.concatenate([hm, hp, hn], axis=0).astype(BF16)
    u = _dot(hext, wu_ref[...])
    z = _dot(hext[:t_rows], wz_ref[...])
    prev_ok = (i > 0).astype(F32)
    next_ok = (i < n_tiles - 1).astype(F32)
    useq_ref[0:POOL_HALO] = u[t_rows:t_rows + POOL_HALO] * prev_ok
    useq_ref[POOL_HALO:POOL_HALO + t_rows] = u[:t_rows]
    useq_ref[POOL_HALO + t_rows:] = u[t_rows + POOL_HALO:] * next_ok

    t = i * t_rows + lax.broadcasted_iota(jnp.int32, (t_rows, 1), 0)
    mixed = []
    for g, w in enumerate(POOL_WINDOWS):
        cols = slice(g * POOL_GROUP_DIM, (g + 1) * POOL_GROUP_DIM)
        acc = useq_ref[pl.ds(POOL_HALO - w // 2, t_rows), cols]
        for d in range(-(w // 2) + 1, w - w // 2):
            acc = acc + useq_ref[pl.ds(POOL_HALO + d, t_rows), cols]
        lo = jnp.clip(t - w // 2, 0, seq_len)
        hi = jnp.clip(t + (w - w // 2), 0, seq_len)
        inv_cnt = 1.0 / (hi - lo).astype(F32)
        centre = useq_ref[pl.ds(POOL_HALO, t_rows), cols]
        mg = (acc * inv_cnt - centre).astype(BF16)
        mixed.append(_dot(mg, wg_ref[g]))
    mixed = jnp.concatenate(mixed, axis=1) * ps_ref[...]
    a = (mixed * _silu(z)).astype(BF16)
    branch = _dot(a, wo_ref[...])
    o_ref[0] = _deepnorm_ln(x, gate, branch, lng_ref[...], lnb_ref[...])


def _pool_layer(x, mod, w_u, w_z, w_grp, p_scale, w_out, ln_g, ln_b):
    bsz, seq_len, _ = x.shape
    t_rows = ROW_TILE
    n_tiles = seq_len // t_rows
    halo_per_tile = t_rows // POOL_HALO
    halo_per_seq = seq_len // POOL_HALO
    n_halo = bsz * halo_per_seq
    xh = x.reshape(n_halo, POOL_HALO, D_MODEL)
    const2 = lambda b, i: (0, 0)
    return pl.pallas_call(
        functools.partial(_pool_kernel, seq_len),
        grid=(bsz, n_tiles),
        in_specs=[
            pl.BlockSpec((1, t_rows, D_MODEL), lambda b, i: (b, i, 0)),
            pl.BlockSpec((1, POOL_HALO, D_MODEL),
                         lambda b, i: (jnp.maximum(b * halo_per_seq + i * halo_per_tile - 1, 0), 0, 0)),
            pl.BlockSpec((1, POOL_HALO, D_MODEL),
                         lambda b, i: (jnp.minimum(b * halo_per_seq + (i + 1) * halo_per_tile, n_halo - 1), 0, 0)),
            pl.BlockSpec((1, 1, 3 * D_MODEL), lambda b, i: (b, 0, 0)),
            pl.BlockSpec((D_MODEL, D_MODEL), const2),
            pl.BlockSpec((D_MODEL, D_MODEL), const2),
            pl.BlockSpec((len(POOL_WINDOWS), POOL_GROUP_DIM, POOL_GROUP_DIM), lambda b, i: (0, 0, 0)),
            pl.BlockSpec((1, D_MODEL), const2),
            pl.BlockSpec((D_MODEL, D_MODEL), const2),
            pl.BlockSpec((1, D_MODEL), const2),
            pl.BlockSpec((1, D_MODEL), const2),
        ],
        out_specs=pl.BlockSpec((1, t_rows, D_MODEL), lambda b, i: (b, i, 0)),
        out_shape=jax.ShapeDtypeStruct(x.shape, F32),
        scratch_shapes=[pltpu.VMEM((t_rows + 2 * POOL_HALO, D_MODEL), F32)],
        compiler_params=_params(2),
        name="pool_layer",
    )(x, xh, xh, mod, w_u, w_z, w_grp, p_scale, w_out, ln_g, ln_b)


def _out_kernel(a_ref, x_ref, mod_ref, wo_ref, lng_ref, lnb_ref, o_ref):
    _, _, gate = _modulation(mod_ref)
    branch = _dot(a_ref[0], wo_ref[...])
    o_ref[0] = _deepnorm_ln(x_ref[0], gate, branch, lng_ref[...], lnb_ref[...])


def _out_layer(a, x, mod, w_out, ln_g, ln_b):
    bsz, seq_len, _ = x.shape
    t_rows = ROW_TILE
    const2 = lambda b, i: (0, 0)
    tok = pl.BlockSpec((1, t_rows, D_MODEL), lambda b, i: (b, i, 0))
    return pl.pallas_call(
        _out_kernel,
        grid=(bsz, seq_len // t_rows),
        in_specs=[tok, tok,
                  pl.BlockSpec((1, 1, 3 * D_MODEL), lambda b, i: (b, 0, 0)),
                  pl.BlockSpec((D_MODEL, D_MODEL), const2),
                  pl.BlockSpec((1, D_MODEL), const2),
                  pl.BlockSpec((1, D_MODEL), const2)],
        out_specs=tok,
        out_shape=jax.ShapeDtypeStruct(x.shape, F32),
        compiler_params=_params(2),
        name="out_layer",
    )(a, x, mod, w_out, ln_g, ln_b)


MLA_W1_COLS = Q_LORA + KV_LORA + 2 * QK_ROPE + MLA_WIDTH
MLA_Z_OFF = Q_LORA + KV_LORA + 2 * QK_ROPE
MLA_Q_COLS = MLA_HEADS * QK_NOPE + MLA_HEADS * LANES
MLA_Q_SCALE = MLA_SCALE * LOG2E


def _rope_block(blk, cs, low_half):
    if cs is not None:
        blk = blk * cs
        blk = blk + pltpu.roll(blk, LANES // 2, 1)
    return jnp.where(low_half, blk, 0.0)


def _mla_proj_kernel(rope, emit_state, *refs):
    x_ref, mod_ref, w1_ref, qn_ref, kvn_ref, wuq_ref, wukv_ref = refs[:7]
    refs = refs[7:]
    cs = None
    if rope:
        cs = refs[0][...]
        refs = refs[1:]
    q_ref, kv_ref, krp_ref, z_ref = refs[:4]
    shift, scale, _ = _modulation(mod_ref)
    h = (x_ref[0] * (1.0 + scale) + shift).astype(BF16)
    p = _dot(h, w1_ref[...])
    cq = p[:, :Q_LORA]
    ckv = p[:, Q_LORA:Q_LORA + KV_LORA]
    krb = p[:, Q_LORA + KV_LORA:MLA_Z_OFF]
    z_ref[0] = p[:, MLA_Z_OFF:]
    ckvn = _rms(ckv, kvn_ref[...])
    if emit_state:
        ckv_ref, kr_ref = refs[4:6]
        ckv_ref[0, 0] = ckvn
        kr_ref[0, 0] = krb[:, :QK_ROPE]
    low_half = lax.broadcasted_iota(jnp.int32, (1, LANES), 1) < LANES // 2
    krp_ref[0] = _rope_block(krb, cs, low_half).astype(BF16)
    kv_ref[0] = _dot(ckvn.astype(BF16), wukv_ref[...]).astype(BF16)
    qf = _dot(_rms(cq, qn_ref[...]).astype(BF16), wuq_ref[...])
    n_nope = MLA_HEADS * QK_NOPE
    q_ref[0, :, :n_nope] = (qf[:, :n_nope] * MLA_Q_SCALE).astype(BF16)
    for hh in range(MLA_HEADS):
        cols = slice(n_nope + hh * LANES, n_nope + (hh + 1) * LANES)
        q_ref[0, :, cols] = (_rope_block(qf[:, cols], cs, low_half) * MLA_Q_SCALE).astype(BF16)


def _mla_proj(x, mod, w1, q_norm, kv_norm, w_uq, w_ukv, rope_cs, emit_state):
    bsz, seq_len, _ = x.shape
    t_rows = ROW_TILE
    const2 = lambda b, i: (0, 0)
    tok = lambda width: pl.BlockSpec((1, t_rows, width), lambda b, i: (b, i, 0))
    in_specs = [tok(D_MODEL),
                pl.BlockSpec((1, 1, 3 * D_MODEL), lambda b, i: (b, 0, 0)),
                pl.BlockSpec((D_MODEL, MLA_W1_COLS), const2),
                pl.BlockSpec((1, Q_LORA), const2),
                pl.BlockSpec((1, KV_LORA), const2),
                pl.BlockSpec((Q_LORA, MLA_Q_COLS), const2),
                pl.BlockSpec((KV_LORA, MLA_HEADS * (QK_NOPE + V_HEAD)), const2)]
    args = [x, mod, w1, q_norm, kv_norm, w_uq, w_ukv]
    if rope_cs is not None:
        in_specs.append(pl.BlockSpec((t_rows, LANES), lambda b, i: (i, 0)))
        args.append(rope_cs)
    out_specs = [tok(MLA_Q_COLS), tok(MLA_HEADS * (QK_NOPE + V_HEAD)), tok(LANES), tok(MLA_WIDTH)]
    out_shape = [jax.ShapeDtypeStruct((bsz, seq_len, MLA_Q_COLS), BF16),
                 jax.ShapeDtypeStruct((bsz, seq_len, MLA_HEADS * (QK_NOPE + V_HEAD)), BF16),
                 jax.ShapeDtypeStruct((bsz, seq_len, LANES), BF16),
                 jax.ShapeDtypeStruct((bsz, seq_len, MLA_WIDTH), F32)]
    if emit_state:
        out_specs += [pl.BlockSpec((1, 1, t_rows, KV_LORA), lambda b, i: (b, 0, i, 0)),
                      pl.BlockSpec((1, 1, t_rows, QK_ROPE), lambda b, i: (b, 0, i, 0))]
        out_shape += [jax.ShapeDtypeStruct((bsz, 1, seq_len, KV_LORA), F32),
                      jax.ShapeDtypeStruct((bsz, 1, seq_len, QK_ROPE), F32)]
    return pl.pallas_call(
        functools.partial(_mla_proj_kernel, rope_cs is not None, emit_state),
        grid=(bsz, seq_len // t_rows),
        in_specs=in_specs,
        out_specs=out_specs,
        out_shape=out_shape,
        compiler_params=_params(2),
        name="mla_proj",
    )(*args)


def _mla_cache_kernel(ckv_ref, kr_ref, wukv_ref, kv_ref, krp_ref):
    kv_ref[0] = _dot(ckv_ref[0, 0].astype(BF16), wukv_ref[...]).astype(BF16)
    krp_ref[0, :, :QK_ROPE] = kr_ref[0, 0].astype(BF16)
    krp_ref[0, :, QK_ROPE:] = jnp.zeros((kr_ref.shape[2], LANES - QK_ROPE), BF16)


def _mla_cache_keys(cache_ckv, cache_kr, slot, w_ukv):
    bsz, _, past, _ = cache_ckv.shape
    kv_cols = MLA_HEADS * (QK_NOPE + V_HEAD)
    return pl.pallas_call(
        _mla_cache_kernel,
        grid=(bsz,),
        in_specs=[pl.BlockSpec((1, 1, past, KV_LORA), lambda b: (b, slot, 0, 0)),
                  pl.BlockSpec((1, 1, past, QK_ROPE), lambda b: (b, slot, 0, 0)),
                  pl.BlockSpec((KV_LORA, kv_cols), lambda b: (0, 0))],
        out_specs=[pl.BlockSpec((1, past, kv_cols), lambda b: (b, 0, 0)),
                   pl.BlockSpec((1, past, LANES), lambda b: (b, 0, 0))],
        out_shape=[jax.ShapeDtypeStruct((bsz, past, kv_cols), BF16),
                   jax.ShapeDtypeStruct((bsz, past, LANES), BF16)],
        compiler_params=_params(1),
        name="mla_cache_keys",
    )(cache_ckv, cache_kr, w_ukv)


def _mla_attn_kernel(n_seg, qn_ref, qr_ref, z_ref, *refs):
    o_ref = refs[3 * n_seg]
    q = jnp.concatenate([qn_ref[0], qr_ref[0]], axis=1)
    scores, values = [], []
    for s in range(n_seg):
        kn_ref, v_ref, kr_ref = refs[3 * s:3 * s + 3]
        k = jnp.concatenate([kn_ref[0], kr_ref[0]], axis=1)
        scores.append(_dot_nt(q, k))
        values.append(v_ref[0])
    m = functools.reduce(jnp.maximum, [jnp.max(s, axis=1, keepdims=True) for s in scores])
    denom = 0.0
    acc = 0.0
    for s, v in zip(scores, values):
        p = jnp.exp2(s - m)
        denom = denom + jnp.sum(p, axis=1, keepdims=True)
        acc = acc + _dot(p.astype(BF16), v)
    o = acc / denom
    o_ref[0] = (o * _silu(z_ref[0])).astype(BF16)


def _mla_attention(q, z, segments, q_tile):
    bsz, lq, _ = q.shape
    in_specs = [pl.BlockSpec((1, q_tile, LANES), lambda b, h, i: (b, i, h)),
                pl.BlockSpec((1, q_tile, LANES), lambda b, h, i: (b, i, MLA_HEADS + h)),
                pl.BlockSpec((1, q_tile, LANES), lambda b, h, i: (b, i, h))]
    args = [q, q, z]
    for kv, krp in segments:
        lk = kv.shape[1]
        in_specs += [pl.BlockSpec((1, lk, LANES), lambda b, h, i: (b, 0, 2 * h)),
                     pl.BlockSpec((1, lk, LANES), lambda b, h, i: (b, 0, 2 * h + 1)),
                     pl.BlockSpec((1, lk, LANES), lambda b, h, i: (b, 0, 0))]
        args += [kv, kv, krp]
    return pl.pallas_call(
        functools.partial(_mla_attn_kernel, len(segments)),
        grid=(bsz, MLA_HEADS, lq // q_tile),
        in_specs=in_specs,
        out_specs=pl.BlockSpec((1, q_tile, LANES), lambda b, h, i: (b, i, h)),
        out_shape=jax.ShapeDtypeStruct((bsz, lq, MLA_WIDTH), BF16),
        compiler_params=_params(3),
        name="mla_attention",
    )(*args)


NA_Q_SCALE = NA_SCALE * LOG2E


def _na_proj_kernel(emit_state, x_ref, mod_ref, w_ref, q_ref, k_ref, v_ref, z_ref, *state_refs):
    shift, scale, _ = _modulation(mod_ref)
    h = (x_ref[0] * (1.0 + scale) + shift).astype(BF16)
    w = NA_WIDTH
    q_ref[0] = (_dot(h, w_ref[:, :w]) * NA_Q_SCALE).astype(BF16)
    k = _dot(h, w_ref[:, w:2 * w])
    v = _dot(h, w_ref[:, 2 * w:3 * w])
    k_ref[0] = k.astype(BF16)
    v_ref[0] = v.astype(BF16)
    z_ref[0] = _dot(h, w_ref[:, 3 * w:])
    if emit_state:
        sk_ref, sv_ref = state_refs
        sk_ref[0, 0] = k
        sv_ref[0, 0] = v


def _na_proj(x, mod, w_in, emit_state):
    bsz, seq_len, _ = x.shape
    t_rows = ROW_TILE
    tok = pl.BlockSpec((1, t_rows, NA_WIDTH), lambda b, i: (b, i, 0))
    out_specs = [tok, tok, tok, tok]
    act = jax.ShapeDtypeStruct((bsz, seq_len, NA_WIDTH), BF16)
    out_shape = [act, act, act, jax.ShapeDtypeStruct((bsz, seq_len, NA_WIDTH), F32)]
    if emit_state:
        st = pl.BlockSpec((1, 1, t_rows, NA_WIDTH), lambda b, i: (b, 0, i, 0))
        out_specs += [st, st]
        out_shape += [jax.ShapeDtypeStruct((bsz, 1, seq_len, NA_WIDTH), F32)] * 2
    return pl.pallas_call(
        functools.partial(_na_proj_kernel, emit_state),
        grid=(bsz, seq_len // t_rows),
        in_specs=[pl.BlockSpec((1, t_rows, D_MODEL), lambda b, i: (b, i, 0)),
                  pl.BlockSpec((1, 1, 3 * D_MODEL), lambda b, i: (b, 0, 0)),
                  pl.BlockSpec((D_MODEL, 4 * NA_WIDTH), lambda b, i: (0, 0))],
        out_specs=out_specs,
        out_shape=out_shape,
        compiler_params=_params(2),
        name="na_proj",
    )(x, mod, w_in)


def _split_head_pair(q2, low_half):
    zero = jnp.zeros_like(q2)
    return jnp.concatenate([jnp.where(low_half, q2, zero), jnp.where(low_half, zero, q2)], axis=0)


def _na_ctx_attn_kernel(q_ref, k_ref, v_ref, z_ref, o_ref):
    n = q_ref.shape[1]
    low_half = lax.broadcasted_iota(jnp.int32, (1, LANES), 1) < NA_HEAD_DIM
    qs = _split_head_pair(q_ref[0], low_half)
    s = _dot_nt(qs, k_ref[0])
    m = jnp.max(s, axis=1, keepdims=True)
    p = jnp.exp2(s - m)
    denom = jnp.sum(p, axis=1, keepdims=True)
    o = _dot(p.astype(BF16), v_ref[0]) / denom
    o2 = jnp.where(low_half, o[:n], o[n:])
    o_ref[0] = (o2 * _silu(z_ref[0])).astype(BF16)


def _na_ctx_attention(q, k, v, z):
    bsz, seq_len, _ = q.shape
    blk = pl.BlockSpec((1, seq_len, LANES), lambda b, j: (b, 0, j))
    return pl.pallas_call(
        _na_ctx_attn_kernel,
        grid=(bsz, NA_PAIRS),
        in_specs=[blk, blk, blk, blk],
        out_specs=blk,
        out_shape=jax.ShapeDtypeStruct((bsz, seq_len, NA_WIDTH), BF16),
        compiler_params=_params(2),
        name="na_ctx_attention",
    )(q, k, v, z)


NA_BIAS_SLOTS = 2 * NA_WIN_R - 2


def _na_bias_kernel(rpb_ref, o_ref):
    h = pl.program_id(0)
    n_dc = 2 * NA_WIN_C - 1
    n_dr = 2 * NA_WIN_R - 1
    qc = lax.broadcasted_iota(jnp.int32, (GRID_W, LANES), 0)
    lane = lax.broadcasted_iota(jnp.int32, (GRID_W, LANES), 1)
    low_half = lane < GRID_W
    kc = jnp.where(low_half, lane, lane - GRID_W)
    cstart = jnp.clip(qc - NA_WIN_C // 2, 0, GRID_W - NA_WIN_C)
    valid = (kc >= cstart) & (kc < cstart + NA_WIN_C)
    dc = kc - qc + (NA_WIN_C - 1)
    for dr in range(NA_BIAS_SLOTS):
        base = (h * n_dr + dr) * n_dc
        tile = jnp.zeros((GRID_W, LANES), F32)
        for d in range(n_dc):
            val = jnp.where(low_half, rpb_ref[base + d], rpb_ref[base + n_dc + d])
            tile = jnp.where(dc == d, val, tile)
        o_ref[0, dr] = jnp.where(valid, tile * LOG2E, NEG_INF)


def _na_bias_table(rpb):
    return pl.pallas_call(
        _na_bias_kernel,
        grid=(NA_HEADS,),
        in_specs=[pl.BlockSpec(memory_space=pltpu.SMEM)],
        out_specs=pl.BlockSpec((1, NA_BIAS_SLOTS, GRID_W, LANES), lambda h: (h, 0, 0, 0)),
        out_shape=jax.ShapeDtypeStruct((NA_HEADS, NA_BIAS_SLOTS, GRID_W, LANES), F32),
        compiler_params=_params(1),
        name="na_bias_table",
    )(rpb.reshape(-1))


def _na_lat_attn_kernel(n_rows, q_ref, k_ref, v_ref, kc_ref, vc_ref, b_ref, z_ref, o_ref):
    r = pl.program_id(2)
    row_start = jnp.clip(r - NA_WIN_R // 2, 0, n_rows - NA_WIN_R)
    off = r - row_start
    n_loc = NA_WIN_R * GRID_W
    low_half = lax.broadcasted_iota(jnp.int32, (1, LANES), 1) < NA_HEAD_DIM
    qs = _split_head_pair(q_ref[0], low_half)
    key0 = pl.multiple_of(row_start * GRID_W, GRID_W)
    kl = k_ref[0, pl.ds(key0, n_loc), :]
    vl = v_ref[0, pl.ds(key0, n_loc), :]
    bias = jnp.concatenate(
        [jnp.concatenate([b_ref[e, 2 * cblk + (NA_WIN_R - 1) - off] for cblk in range(NA_WIN_R // 2)], axis=1)
         for e in range(2)], axis=0)
    s_loc = _dot_nt(qs, kl) + bias
    s_ctx = _dot_nt(qs, kc_ref[0])
    m = jnp.maximum(jnp.max(s_loc, axis=1, keepdims=True), jnp.max(s_ctx, axis=1, keepdims=True))
    p_loc = jnp.exp2(s_loc - m)
    p_ctx = jnp.exp2(s_ctx - m)
    denom = jnp.sum(p_loc, axis=1, keepdims=True) + jnp.sum(p_ctx, axis=1, keepdims=True)
    o = (_dot(p_loc.astype(BF16), vl) + _dot(p_ctx.astype(BF16), vc_ref[0])) / denom
    o2 = jnp.where(low_half, o[:GRID_W], o[GRID_W:])
    o_ref[0] = (o2 * _silu(z_ref[0])).astype(BF16)


def _na_lat_attention(q, k, v, k_ctx, v_ctx, bias, z):
    bsz, seq_len, _ = q.shape
    n_rows = seq_len // GRID_W
    past = k_ctx.shape[1]
    row_blk = pl.BlockSpec((1, GRID_W, LANES), lambda b, j, r: (b, r, j))
    seq_blk = pl.BlockSpec((1, seq_len, LANES), lambda b, j, r: (b, 0, j))
    ctx_blk = pl.BlockSpec((1, past, LANES), lambda b, j, r: (b, 0, j))
    return pl.pallas_call(
        functools.partial(_na_lat_attn_kernel, n_rows),
        grid=(bsz, NA_PAIRS, n_rows),
        in_specs=[row_blk, seq_blk, seq_blk, ctx_blk, ctx_blk,
                  pl.BlockSpec((2, NA_BIAS_SLOTS, GRID_W, LANES), lambda b, j, r: (j, 0, 0, 0)),
                  row_blk],
        out_specs=row_blk,
        out_shape=jax.ShapeDtypeStruct((bsz, seq_len, NA_WIDTH), BF16),
        compiler_params=_params(3),
        name="na_lat_attention",
    )(q, k, v, k_ctx, v_ctx, bias, z)


def _rope_swap_perm():
    nf = QK_ROPE // 4
    idx = np.arange(QK_ROPE).reshape(2, 2, nf)
    return idx[:, ::-1, :].reshape(-1)


def _rope_table(seq_len):
    nf = QK_ROPE // 4
    inv = ROPE_BASE ** (-jnp.arange(nf, dtype=F32) / nf)
    t = jnp.arange(seq_len)
    pos = jnp.stack([t // GRID_W, t % GRID_W], -1).astype(F32)
    ang = pos[:, :, None] * inv
    cos = jnp.cos(ang)
    sin = jnp.sin(ang)
    c = jnp.concatenate([cos, cos], axis=-1).reshape(seq_len, QK_ROPE)
    s = jnp.concatenate([-sin, sin], axis=-1).reshape(seq_len, QK_ROPE)
    return jnp.concatenate([c, s], axis=-1)


def _mla_weights(w_in, w_uq):
    perm = _rope_swap_perm()
    o1, o2, o3 = Q_LORA, Q_LORA + KV_LORA, Q_LORA + KV_LORA + QK_ROPE
    w_kr = w_in[:, o2:o3]
    w1 = jnp.concatenate([w_in[:, :o2], w_kr, w_kr[:, perm], w_in[:, o3:]], axis=1).astype(BF16)
    wq = w_uq.reshape(Q_LORA, MLA_HEADS, QK_NOPE + QK_ROPE)
    nope = wq[:, :, :QK_NOPE].reshape(Q_LORA, MLA_HEADS * QK_NOPE)
    rope = wq[:, :, QK_NOPE:]
    rope_blk = jnp.concatenate([rope, rope[:, :, perm]], axis=-1).reshape(Q_LORA, MLA_HEADS * LANES)
    return w1, jnp.concatenate([nope, rope_blk], axis=1).astype(BF16)


def kernel(x_prompt, x_sample, cache_mla_ckv, cache_mla_krope, cache_na_k, cache_na_v, c, c_ctx, ada_w, ada_b, ln_g, ln_b, pool_w_in, pool_w_grp, pool_scale, pool_w_out, mla_w_in, mla_q_norm, mla_w_uq, mla_kv_norm, mla_w_ukv, mla_w_out, na_w_in, na_rpb, na_w_out):
    n_ctx, n_lat = x_prompt.shape[0], x_sample.shape[0]
    lat_len = x_sample.shape[1]
    cond = jnp.concatenate([c_ctx[None], c, jnp.zeros((8 - 1 - n_lat, D_MODEL), F32)], axis=0)
    mod = _ada(cond, ada_w, ada_b)
    yp, ys = x_prompt, x_sample
    st_ckv = st_kr = st_k = st_v = None
    for i in range(DEPTH):
        kind, j = i % N_MIXERS, i // N_MIXERS
        mod_p = jnp.broadcast_to(mod[i, 0:1][None], (n_ctx, 1, 3 * D_MODEL))
        mod_s = mod[i, 1:1 + n_lat][:, None, :]
        g, b = ln_g[i][None], ln_b[i][None]
        if kind == 0:
            w_u = pool_w_in[j][:, :D_MODEL].astype(BF16)
            w_z = pool_w_in[j][:, D_MODEL:].astype(BF16)
            w_g = pool_w_grp[j].astype(BF16)
            w_o = pool_w_out[j].astype(BF16)
            ps = pool_scale[j][None]
            yp = _pool_layer(yp, mod_p, w_u, w_z, w_g, ps, w_o, g, b)
            ys = _pool_layer(ys, mod_s, w_u, w_z, w_g, ps, w_o, g, b)
        elif kind == 1:
            w1, w_uq = _mla_weights(mla_w_in[j], mla_w_uq[j])
            w_ukv = mla_w_ukv[j].astype(BF16)
            w_o = mla_w_out[j].astype(BF16)
            qn, kvn = mla_q_norm[j][None], mla_kv_norm[j][None]
            q_p, kv_p, krp_p, z_p, st_ckv, st_kr = _mla_proj(yp, mod_p, w1, qn, kvn, w_uq, w_ukv, None, True)
            a_p = _mla_attention(q_p, z_p, [(kv_p, krp_p)], ROW_TILE)
            yp = _out_layer(a_p, yp, mod_p, w_o, g, b)
            q_s, kv_s, krp_s, z_s = _mla_proj(ys, mod_s, w1, qn, kvn, w_uq, w_ukv, _rope_table(lat_len), False)
            kv_c, krp_c = _mla_cache_keys(cache_mla_ckv, cache_mla_krope, j, w_ukv)
            a_s = _mla_attention(q_s, z_s, [(kv_c, krp_c), (kv_s, krp_s)], ROW_TILE)
            ys = _out_layer(a_s, ys, mod_s, w_o, g, b)
        else:
            w_in = na_w_in[j].astype(BF16)
            w_o = na_w_out[j].astype(BF16)
            q_p, k_p, v_p, z_p, st_k, st_v = _na_proj(yp, mod_p, w_in, True)
            a_p = _na_ctx_attention(q_p, k_p, v_p, z_p)
            yp = _out_layer(a_p, yp, mod_p, w_o, g, b)
            q_s, k_s, v_s, z_s = _na_proj(ys, mod_s, w_in, False)
            past = cache_na_k.shape[2]
            k_c = cache_na_k[:, j].reshape(n_lat, past, NA_WIDTH).astype(BF16)
            v_c = cache_na_v[:, j].reshape(n_lat, past, NA_WIDTH).astype(BF16)
            bias = _na_bias_table(na_rpb[j])
            a_s = _na_lat_attention(q_s, k_s, v_s, k_c, v_c, bias, z_s)
            ys = _out_layer(a_s, ys, mod_s, w_o, g, b)
    state_na_k = st_k.reshape(n_ctx, 1, -1, NA_HEADS, NA_HEAD_DIM)
    state_na_v = st_v.reshape(n_ctx, 1, -1, NA_HEADS, NA_HEAD_DIM)
    return (yp, ys, st_ckv, st_kr, state_na_k, state_na_v)
```

```python
import functools
import math

import numpy as np
import jax
import jax.numpy as jnp
from jax import lax
from jax.experimental import pallas as pl
from jax.experimental.pallas import tpu as pltpu

F32 = jnp.float32
BF16 = jnp.bfloat16

D_MODEL = 1024
DEPTH = 4
GRID_W = 64
N_MIXERS = 3

POOL_WINDOWS = (2, 4, 8, 16)
POOL_GROUP_DIM = D_MODEL // len(POOL_WINDOWS)
POOL_HALO = 8

MLA_HEADS = 8
QK_NOPE = 128
QK_ROPE = 64
V_HEAD = 128
Q_LORA = D_MODEL // 2
KV_LORA = D_MODEL // 4
MLA_WIDTH = MLA_HEADS * V_HEAD
MLA_SCALE = (QK_NOPE + QK_ROPE) ** -0.5

NA_HEADS = 16
NA_HEAD_DIM = 64
NA_WIDTH = NA_HEADS * NA_HEAD_DIM
NA_WIN_R = 8
NA_WIN_C = 16
NA_SCALE = NA_HEAD_DIM ** -0.5
NA_PAIRS = NA_HEADS // 2

ROPE_BASE = 10000.0
LN_EPS = 1e-5
RMS_EPS = 1e-6
NEG_INF = -1e30
DEEPNORM_ALPHA = (2 * DEPTH) ** 0.25
LOG2E = math.log2(math.e)

LANES = 128
ROW_TILE = 256
VMEM_LIMIT = 56 * 1024 * 1024


def _params(n_axes):
    return pltpu.CompilerParams(dimension_semantics=("arbitrary",) * n_axes, vmem_limit_bytes=VMEM_LIMIT)


def _silu(x):
    return x * jax.nn.sigmoid(x)


def _dot(a, b):
    return jnp.dot(a, b, preferred_element_type=F32)


def _dot_nt(a, b):
    return lax.dot_general(a, b, (((1,), (1,)), ((), ())), preferred_element_type=F32)


def _modulation(mod_ref):
    m = mod_ref[0]
    return m[:, :D_MODEL], m[:, D_MODEL:2 * D_MODEL], m[:, 2 * D_MODEL:]


def _deepnorm_ln(x, gate, branch, g, b):
    xf = DEEPNORM_ALPHA * x + gate * branch
    mu = jnp.mean(xf, axis=-1, keepdims=True)
    xc = xf - mu
    var = jnp.mean(xc * xc, axis=-1, keepdims=True)
    return xc * lax.rsqrt(var + LN_EPS) * g + b


def _rms(x, g):
    return x * lax.rsqrt(jnp.mean(x * x, axis=-1, keepdims=True) + RMS_EPS) * g


def _ada_kernel(cond_ref, w_ref, b_ref, o_ref):
    c = cond_ref[...]
    o_ref[0] = _dot(_silu(c).astype(BF16), w_ref[0].astype(BF16)) + b_ref[0]


def _ada(cond, ada_w, ada_b):
    tn = D_MODEL
    return pl.pallas_call(
        _ada_kernel,
        grid=(DEPTH, 3 * D_MODEL // tn),
        in_specs=[
            pl.BlockSpec((8, D_MODEL), lambda l, n: (0, 0)),
            pl.BlockSpec((1, D_MODEL, tn), lambda l, n: (l, 0, n)),
            pl.BlockSpec((1, 1, tn), lambda l, n: (l, 0, n)),
        ],
        out_specs=pl.BlockSpec((1, 8, tn), lambda l, n: (l, 0, n)),
        out_shape=jax.ShapeDtypeStruct((DEPTH, 8, 3 * D_MODEL), F32),
        compiler_params=_params(2),
        name="ada",
    )(cond, ada_w, ada_b.reshape(DEPTH, 1, 3 * D_MODEL))


def _pool_kernel(seq_len, x_ref, xp_ref, xn_ref, mod_ref, wu_ref, wz_ref, wg_ref, ps_ref, wo_ref,
                 lng_ref, lnb_ref, o_ref, useq_ref):
    t_rows = x_ref.shape[1]
    i = pl.program_id(1)
    n_tiles = pl.num_programs(1)
    shift, scale, gate = _modulation(mod_ref)
    one_scale = 1.0 + scale
    x = x_ref[0]
    hm = x * one_scale + shift
    hp = xp_ref[0] * one_scale + shift
    hn = xn_ref[0] * one_scale + shift
    hext = jnp.concatenate([hm, hp, hn], axis=0).astype(BF16)
    u = _dot(hext, wu_ref[...])
    z = _dot(hext[:t_rows], wz_ref[...])
    prev_ok = (i > 0).astype(F32)
    next_ok = (i < n_tiles - 1).astype(F32)
    useq_ref[0:POOL_HALO] = u[t_rows:t_rows + POOL_HALO] * prev_ok
    useq_ref[POOL_HALO:POOL_HALO + t_rows] = u[:t_rows]
    useq_ref[POOL_HALO + t_rows:] = u[t_rows + POOL_HALO:] * next_ok

    t = i * t_rows + lax.broadcasted_iota(jnp.int32, (t_rows, 1), 0)
    mixed = []
    for g, w in enumerate(POOL_WINDOWS):
        cols = slice(g * POOL_GROUP_DIM, (g + 1) * POOL_GROUP_DIM)
        acc = useq_ref[pl.ds(POOL_HALO - w // 2, t_rows), cols]
        for d in range(-(w // 2) + 1, w - w // 2):
            acc = acc + useq_ref[pl.ds(POOL_HALO + d, t_rows), cols]
        lo = jnp.clip(t - w // 2, 0, seq_len)
        hi = jnp.clip(t + (w - w // 2), 0, seq_len)
        inv_cnt = 1.0 / (hi - lo).astype(F32)
        centre = useq_ref[pl.ds(POOL_HALO, t_rows), cols]
        mg = (acc * inv_cnt - centre).astype(BF16)
        mixed.append(_dot(mg, wg_ref[g]))
    mixed = jnp.concatenate(mixed, axis=1) * ps_ref[...]
    a = (mixed * _silu(z)).astype(BF16)
    branch = _dot(a, wo_ref[...])
    o_ref[0] = _deepnorm_ln(x, gate, branch, lng_ref[...], lnb_ref[...])


def _pool_layer(x, mod, w_u, w_z, w_grp, p_scale, w_out, ln_g, ln_b):
    bsz, seq_len, _ = x.shape
    t_rows = ROW_TILE
    n_tiles = seq_len // t_rows
    halo_per_tile = t_rows // POOL_HALO
    halo_per_seq = seq_len // POOL_HALO
    n_halo = bsz * halo_per_seq
    xh = x.reshape(n_halo, POOL_HALO, D_MODEL)
    const2 = lambda b, i: (0, 0)
    return pl.pallas_call(
        functools.partial(_pool_kernel, seq_len),
        grid=(bsz, n_tiles),
        in_specs=[
            pl.BlockSpec((1, t_rows, D_MODEL), lambda b, i: (b, i, 0)),
            pl.BlockSpec((1, POOL_HALO, D_MODEL),
                         lambda b, i: (jnp.maximum(b * halo_per_seq + i * halo_per_tile - 1, 0), 0, 0)),
            pl.BlockSpec((1, POOL_HALO, D_MODEL),
                         lambda b, i: (jnp.minimum(b * halo_per_seq + (i + 1) * halo_per_tile, n_halo - 1), 0, 0)),
            pl.BlockSpec((1, 1, 3 * D_MODEL), lambda b, i: (b, 0, 0)),
            pl.BlockSpec((D_MODEL, D_MODEL), const2),
            pl.BlockSpec((D_MODEL, D_MODEL), const2),
            pl.BlockSpec((len(POOL_WINDOWS), POOL_GROUP_DIM, POOL_GROUP_DIM), lambda b, i: (0, 0, 0)),
            pl.BlockSpec((1, D_MODEL), const2),
            pl.BlockSpec((D_MODEL, D_MODEL), const2),
            pl.BlockSpec((1, D_MODEL), const2),
            pl.BlockSpec((1, D_MODEL), const2),
        ],
        out_specs=pl.BlockSpec((1, t_rows, D_MODEL), lambda b, i: (b, i, 0)),
        out_shape=jax.ShapeDtypeStruct(x.shape, F32),
        scratch_shapes=[pltpu.VMEM((t_rows + 2 * POOL_HALO, D_MODEL), F32)],
        compiler_params=_params(2),
        name="pool_layer",
    )(x, xh, xh, mod, w_u, w_z, w_grp, p_scale, w_out, ln_g, ln_b)


def _out_kernel(a_ref, x_ref, mod_ref, wo_ref, lng_ref, lnb_ref, o_ref):
    _, _, gate = _modulation(mod_ref)
    branch = _dot(a_ref[0], wo_ref[...])
    o_ref[0] = _deepnorm_ln(x_ref[0], gate, branch, lng_ref[...], lnb_ref[...])


def _out_layer(a, x, mod, w_out, ln_g, ln_b):
    bsz, seq_len, _ = x.shape
    t_rows = ROW_TILE
    const2 = lambda b, i: (0, 0)
    tok = pl.BlockSpec((1, t_rows, D_MODEL), lambda b, i: (b, i, 0))
    return pl.pallas_call(
        _out_kernel,
        grid=(bsz, seq_len // t_rows),
        in_specs=[tok, tok,
                  pl.BlockSpec((1, 1, 3 * D_MODEL), lambda b, i: (b, 0, 0)),
                  pl.BlockSpec((D_MODEL, D_MODEL), const2),
                  pl.BlockSpec((1, D_MODEL), const2),
                  pl.BlockSpec((1, D_MODEL), const2)],
        out_specs=tok,
        out_shape=jax.ShapeDtypeStruct(x.shape, F32),
        compiler_params=_params(2),
        name="out_layer",
    )(a, x, mod, w_out, ln_g, ln_b)


MLA_W1_COLS = Q_LORA + KV_LORA + 2 * QK_ROPE + MLA_WIDTH
MLA_Z_OFF = Q_LORA + KV_LORA + 2 * QK_ROPE
MLA_Q_COLS = MLA_HEADS * QK_NOPE + MLA_HEADS * LANES
MLA_Q_SCALE = MLA_SCALE * LOG2E


def _rope_block(blk, cs, low_half):
    if cs is not None:
        blk = blk * cs
        blk = blk + pltpu.roll(blk, LANES // 2, 1)
    return jnp.where(low_half, blk, 0.0)


def _mla_proj_kernel(rope, emit_state, *refs):
    x_ref, mod_ref, w1_ref, qn_ref, kvn_ref, wuq_ref, wukv_ref = refs[:7]
    refs = refs[7:]
    cs = None
    if rope:
        cs = refs[0][...]
        refs = refs[1:]
    q_ref, kv_ref, krp_ref, z_ref = refs[:4]
    shift, scale, _ = _modulation(mod_ref)
    h = (x_ref[0] * (1.0 + scale) + shift).astype(BF16)
    p = _dot(h, w1_ref[...])
    cq = p[:, :Q_LORA]
    ckv = p[:, Q_LORA:Q_LORA + KV_LORA]
    krb = p[:, Q_LORA + KV_LORA:MLA_Z_OFF]
    z_ref[0] = p[:, MLA_Z_OFF:]
    ckvn = _rms(ckv, kvn_ref[...])
    if emit_state:
        ckv_ref, kr_ref = refs[4:6]
        ckv_ref[0, 0] = ckvn
        kr_ref[0, 0] = krb[:, :QK_ROPE]
    low_half = lax.broadcasted_iota(jnp.int32, (1, LANES), 1) < LANES // 2
    krp_ref[0] = _rope_block(krb, cs, low_half).astype(BF16)
    kv_ref[0] = _dot(ckvn.astype(BF16), wukv_ref[...]).astype(BF16)
    qf = _dot(_rms(cq, qn_ref[...]).astype(BF16), wuq_ref[...])
    n_nope = MLA_HEADS * QK_NOPE
    q_ref[0, :, :n_nope] = (qf[:, :n_nope] * MLA_Q_SCALE).astype(BF16)
    for hh in range(MLA_HEADS):
        cols = slice(n_nope + hh * LANES, n_nope + (hh + 1) * LANES)
        q_ref[0, :, cols] = (_rope_block(qf[:, cols], cs, low_half) * MLA_Q_SCALE).astype(BF16)


def _mla_proj(x, mod, w1, q_norm, kv_norm, w_uq, w_ukv, rope_cs, emit_state):
    bsz, seq_len, _ = x.shape
    t_rows = ROW_TILE
    const2 = lambda b, i: (0, 0)
    tok = lambda width: pl.BlockSpec((1, t_rows, width), lambda b, i: (b, i, 0))
    in_specs = [tok(D_MODEL),
                pl.BlockSpec((1, 1, 3 * D_MODEL), lambda b, i: (b, 0, 0)),
                pl.BlockSpec((D_MODEL, MLA_W1_COLS), const2),
                pl.BlockSpec((1, Q_LORA), const2),
                pl.BlockSpec((1, KV_LORA), const2),
                pl.BlockSpec((Q_LORA, MLA_Q_COLS), const2),
                pl.BlockSpec((KV_LORA, MLA_HEADS * (QK_NOPE + V_HEAD)), const2)]
    args = [x, mod, w1, q_norm, kv_norm, w_uq, w_ukv]
    if rope_cs is not None:
        in_specs.append(pl.BlockSpec((t_rows, LANES), lambda b, i: (i, 0)))
        args.append(rope_cs)
    out_specs = [tok(MLA_Q_COLS), tok(MLA_HEADS * (QK_NOPE + V_HEAD)), tok(LANES), tok(MLA_WIDTH)]
    out_shape = [jax.ShapeDtypeStruct((bsz, seq_len, MLA_Q_COLS), BF16),
                 jax.ShapeDtypeStruct((bsz, seq_len, MLA_HEADS * (QK_NOPE + V_HEAD)), BF16),
                 jax.ShapeDtypeStruct((bsz, seq_len, LANES), BF16),
                 jax.ShapeDtypeStruct((bsz, seq_len, MLA_WIDTH), F32)]
    if emit_state:
        out_specs += [pl.BlockSpec((1, 1, t_rows, KV_LORA), lambda b, i: (b, 0, i, 0)),
                      pl.BlockSpec((1, 1, t_rows, QK_ROPE), lambda b, i: (b, 0, i, 0))]
        out_shape += [jax.ShapeDtypeStruct((bsz, 1, seq_len, KV_LORA), F32),
                      jax.ShapeDtypeStruct((bsz, 1, seq_len, QK_ROPE), F32)]
    return pl.pallas_call(
        functools.partial(_mla_proj_kernel, rope_cs is not None, emit_state),
        grid=(bsz, seq_len // t_rows),
        in_specs=in_specs,
        out_specs=out_specs,
        out_shape=out_shape,
        compiler_params=_params(2),
        name="mla_proj",
    )(*args)


def _mla_cache_kernel(ckv_ref, kr_ref, wukv_ref, kv_ref, krp_ref):
    kv_ref[0] = _dot(ckv_ref[0, 0].astype(BF16), wukv_ref[...]).astype(BF16)
    krp_ref[0, :, :QK_ROPE] = kr_ref[0, 0].astype(BF16)
    krp_ref[0, :, QK_ROPE:] = jnp.zeros((kr_ref.shape[2], LANES - QK_ROPE), BF16)


def _mla_cache_keys(cache_ckv, cache_kr, slot, w_ukv):
    bsz, _, past, _ = cache_ckv.shape
    kv_cols = MLA_HEADS * (QK_NOPE + V_HEAD)
    return pl.pallas_call(
        _mla_cache_kernel,
        grid=(bsz,),
        in_specs=[pl.BlockSpec((1, 1, past, KV_LORA), lambda b: (b, slot, 0, 0)),
                  pl.BlockSpec((1, 1, past, QK_ROPE), lambda b: (b, slot, 0, 0)),
                  pl.BlockSpec((KV_LORA, kv_cols), lambda b: (0, 0))],
        out_specs=[pl.BlockSpec((1, past, kv_cols), lambda b: (b, 0, 0)),
                   pl.BlockSpec((1, past, LANES), lambda b: (b, 0, 0))],
        out_shape=[jax.ShapeDtypeStruct((bsz, past, kv_cols), BF16),
                   jax.ShapeDtypeStruct((bsz, past, LANES), BF16)],
        compiler_params=_params(1),
        name="mla_cache_keys",
    )(cache_ckv, cache_kr, w_ukv)


def _softmax_pv(scores, values):
    m = functools.reduce(jnp.maximum, [jnp.max(s, axis=1, keepdims=True) for s in scores])
    denom = 0.0
    acc = 0.0
    for s, v in zip(scores, values):
        p = jnp.exp2(s - m)
        denom = denom + jnp.sum(p, axis=1, keepdims=True)
        acc = acc + _dot(p.astype(BF16), v)
    return acc / denom


def _mla_attn_kernel(n_seg, heads, sub_rows, qn_ref, qr_ref, z_ref, *refs):
    o_ref = refs[2 * n_seg]
    q_rows = qn_ref.shape[1]
    kv_w = QK_NOPE + V_HEAD
    for h in range(heads):
        lanes = slice(h * LANES, (h + 1) * LANES)
        keys, values = [], []
        for s in range(n_seg):
            kv_ref, kr_ref = refs[2 * s:2 * s + 2]
            keys.append(jnp.concatenate([kv_ref[0, :, h * kv_w:h * kv_w + QK_NOPE], kr_ref[0]], axis=1))
            values.append(kv_ref[0, :, h * kv_w + QK_NOPE:(h + 1) * kv_w])
        for r0 in range(0, q_rows, sub_rows):
            rows = slice(r0, r0 + sub_rows)
            q = jnp.concatenate([qn_ref[0, rows, lanes], qr_ref[0, rows, lanes]], axis=1)
            o = _softmax_pv([_dot_nt(q, k) for k in keys], values)
            o_ref[0, rows, lanes] = (o * _silu(z_ref[0, rows, lanes])).astype(BF16)


def _mla_attention(q, z, segments, q_tile, heads, sub_rows):
    bsz, lq, _ = q.shape
    n_hblk = MLA_HEADS // heads
    hw = heads * LANES
    in_specs = [pl.BlockSpec((1, q_tile, hw), lambda b, h, i: (b, i, h)),
                pl.BlockSpec((1, q_tile, hw), lambda b, h, i: (b, i, n_hblk + h)),
                pl.BlockSpec((1, q_tile, hw), lambda b, h, i: (b, i, h))]
    args = [q, q, z]
    for kv, krp in segments:
        lk = kv.shape[1]
        in_specs += [pl.BlockSpec((1, lk, 2 * hw), lambda b, h, i: (b, 0, h)),
                     pl.BlockSpec((1, lk, LANES), lambda b, h, i: (b, 0, 0))]
        args += [kv, krp]
    return pl.pallas_call(
        functools.partial(_mla_attn_kernel, len(segments), heads, sub_rows),
        grid=(bsz, n_hblk, lq // q_tile),
        in_specs=in_specs,
        out_specs=pl.BlockSpec((1, q_tile, hw), lambda b, h, i: (b, i, h)),
        out_shape=jax.ShapeDtypeStruct((bsz, lq, MLA_WIDTH), BF16),
        compiler_params=_params(3),
        name="mla_attention",
    )(*args)


NA_Q_SCALE = NA_SCALE * LOG2E


def _na_proj_kernel(emit_state, x_ref, mod_ref, w_ref, q_ref, k_ref, v_ref, z_ref, *state_refs):
    shift, scale, _ = _modulation(mod_ref)
    h = (x_ref[0] * (1.0 + scale) + shift).astype(BF16)
    w = NA_WIDTH
    q_ref[0] = (_dot(h, w_ref[:, :w]) * NA_Q_SCALE).astype(BF16)
    k = _dot(h, w_ref[:, w:2 * w])
    v = _dot(h, w_ref[:, 2 * w:3 * w])
    k_ref[0] = k.astype(BF16)
    v_ref[0] = v.astype(BF16)
    z_ref[0] = _dot(h, w_ref[:, 3 * w:])
    if emit_state:
        sk_ref, sv_ref = state_refs
        sk_ref[0, 0] = k
        sv_ref[0, 0] = v


def _na_proj(x, mod, w_in, emit_state):
    bsz, seq_len, _ = x.shape
    t_rows = ROW_TILE
    tok = pl.BlockSpec((1, t_rows, NA_WIDTH), lambda b, i: (b, i, 0))
    out_specs = [tok, tok, tok, tok]
    act = jax.ShapeDtypeStruct((bsz, seq_len, NA_WIDTH), BF16)
    out_shape = [act, act, act, jax.ShapeDtypeStruct((bsz, seq_len, NA_WIDTH), F32)]
    if emit_state:
        st = pl.BlockSpec((1, 1, t_rows, NA_WIDTH), lambda b, i: (b, 0, i, 0))
        out_specs += [st, st]
        out_shape += [jax.ShapeDtypeStruct((bsz, 1, seq_len, NA_WIDTH), F32)] * 2
    return pl.pallas_call(
        functools.partial(_na_proj_kernel, emit_state),
        grid=(bsz, seq_len // t_rows),
        in_specs=[pl.BlockSpec((1, t_rows, D_MODEL), lambda b, i: (b, i, 0)),
                  pl.BlockSpec((1, 1, 3 * D_MODEL), lambda b, i: (b, 0, 0)),
                  pl.BlockSpec((D_MODEL, 4 * NA_WIDTH), lambda b, i: (0, 0))],
        out_specs=out_specs,
        out_shape=out_shape,
        compiler_params=_params(2),
        name="na_proj",
    )(x, mod, w_in)


def _split_head_pair(q2, low_half):
    zero = jnp.zeros_like(q2)
    return jnp.concatenate([jnp.where(low_half, q2, zero), jnp.where(low_half, zero, q2)], axis=0)


def _na_ctx_attn_kernel(q_ref, k_ref, v_ref, z_ref, o_ref):
    n = q_ref.shape[1]
    low_half = lax.broadcasted_iota(jnp.int32, (1, LANES), 1) < NA_HEAD_DIM
    for j in range(NA_PAIRS):
        lanes = slice(j * LANES, (j + 1) * LANES)
        qs = _split_head_pair(q_ref[0, :, lanes], low_half)
        o = _softmax_pv([_dot_nt(qs, k_ref[0, :, lanes])], [v_ref[0, :, lanes]])
        o2 = jnp.where(low_half, o[:n], o[n:])
        o_ref[0, :, lanes] = (o2 * _silu(z_ref[0, :, lanes])).astype(BF16)


def _na_ctx_attention(q, k, v, z):
    bsz, seq_len, _ = q.shape
    blk = pl.BlockSpec((1, seq_len, NA_WIDTH), lambda b: (b, 0, 0))
    return pl.pallas_call(
        _na_ctx_attn_kernel,
        grid=(bsz,),
        in_specs=[blk, blk, blk, blk],
        out_specs=blk,
        out_shape=jax.ShapeDtypeStruct((bsz, seq_len, NA_WIDTH), BF16),
        compiler_params=_params(1),
        name="na_ctx_attention",
    )(q, k, v, z)


NA_N_DR = 2 * NA_WIN_R - 1
NA_BIAS_SLOTS = NA_N_DR + 1


def _na_bias_kernel(rpb_ref, o_ref):
    h = pl.program_id(0)
    n_dc = 2 * NA_WIN_C - 1
    qc = lax.broadcasted_iota(jnp.int32, (GRID_W, LANES), 0)
    lane = lax.broadcasted_iota(jnp.int32, (GRID_W, LANES), 1)
    low_half = lane < GRID_W
    kc = jnp.where(low_half, lane, lane - GRID_W)
    cstart = jnp.clip(qc - NA_WIN_C // 2, 0, GRID_W - NA_WIN_C)
    valid = (kc >= cstart) & (kc < cstart + NA_WIN_C)
    dc = kc - qc + (NA_WIN_C - 1)
    for i in range(NA_BIAS_SLOTS):
        base_l = (h * NA_N_DR + max(i - 1, 0)) * n_dc
        base_r = (h * NA_N_DR + min(i, NA_N_DR - 1)) * n_dc
        tile = jnp.zeros((GRID_W, LANES), F32)
        for d in range(n_dc):
            val = jnp.where(low_half, rpb_ref[base_l + d], rpb_ref[base_r + d])
            tile = jnp.where(dc == d, val, tile)
        o_ref[0, i] = jnp.where(valid, tile * LOG2E, NEG_INF)


def _na_bias_table(rpb):
    return pl.pallas_call(
        _na_bias_kernel,
        grid=(NA_HEADS,),
        in_specs=[pl.BlockSpec(memory_space=pltpu.SMEM)],
        out_specs=pl.BlockSpec((1, NA_BIAS_SLOTS, GRID_W, LANES), lambda h: (h, 0, 0, 0)),
        out_shape=jax.ShapeDtypeStruct((NA_HEADS, NA_BIAS_SLOTS, GRID_W, LANES), F32),
        compiler_params=_params(1),
        name="na_bias_table",
    )(rpb.reshape(-1))


NA_ROWS_PER_GROUP = 4
NA_BAND_ROWS = NA_ROWS_PER_GROUP + NA_WIN_R
NA_GROUPS_PER_STEP = 2


def _na_lat_attn_kernel(n_rows, q_ref, k_ref, v_ref, kc_ref, vc_ref, b_ref, z_ref, o_ref):
    n_q = NA_ROWS_PER_GROUP * GRID_W
    low_half = lax.broadcasted_iota(jnp.int32, (1, LANES), 1) < NA_HEAD_DIM
    for grp in range(NA_GROUPS_PER_STEP):
        rows = slice(grp * n_q, (grp + 1) * n_q)
        r0 = (pl.program_id(2) * NA_GROUPS_PER_STEP + grp) * NA_ROWS_PER_GROUP
        band0 = jnp.clip(r0 - NA_WIN_R // 2, 0, n_rows - NA_BAND_ROWS)
        qs = _split_head_pair(q_ref[0, rows, :], low_half)
        key0 = pl.multiple_of(band0 * GRID_W, GRID_W)
        kl = k_ref[0, pl.ds(key0, NA_BAND_ROWS * GRID_W), :]
        vl = v_ref[0, pl.ds(key0, NA_BAND_ROWS * GRID_W), :]
        bias_rows = []
        for rr in range(NA_ROWS_PER_GROUP):
            r = r0 + rr
            win0 = jnp.clip(r - NA_WIN_R // 2, 0, n_rows - NA_WIN_R)
            blocks = []
            for cblk in range(NA_BAND_ROWS // 2):
                key_row = band0 + 2 * cblk
                slot = jnp.clip(key_row - r + NA_WIN_R, 0, NA_BIAS_SLOTS - 1)
                left_pen = jnp.where((key_row >= win0) & (key_row < win0 + NA_WIN_R), 0.0, NEG_INF)
                right_pen = jnp.where((key_row + 1 >= win0) & (key_row + 1 < win0 + NA_WIN_R), 0.0, NEG_INF)
                blocks.append((slot, jnp.where(low_half, left_pen, right_pen)))
            bias_rows.append(blocks)
        bias = jnp.concatenate(
            [jnp.concatenate(
                [jnp.concatenate([b_ref[e, slot] + pen for slot, pen in blocks], axis=1)
                 for blocks in bias_rows], axis=0)
             for e in range(2)], axis=0)
        o = _softmax_pv([_dot_nt(qs, kl) + bias, _dot_nt(qs, kc_ref[0])], [vl, vc_ref[0]])
        o2 = jnp.where(low_half, o[:n_q], o[n_q:])
        o_ref[0, rows, :] = (o2 * _silu(z_ref[0, rows, :])).astype(BF16)


def _na_lat_attention(q, k, v, k_ctx, v_ctx, bias, z):
    bsz, seq_len, _ = q.shape
    n_rows = seq_len // GRID_W
    past = k_ctx.shape[1]
    rows_per_step = NA_GROUPS_PER_STEP * NA_ROWS_PER_GROUP
    row_blk = pl.BlockSpec((1, rows_per_step * GRID_W, LANES), lambda b, j, r: (b, r, j))
    seq_blk = pl.BlockSpec((1, seq_len, LANES), lambda b, j, r: (b, 0, j))
    ctx_blk = pl.BlockSpec((1, past, LANES), lambda b, j, r: (b, 0, j))
    return pl.pallas_call(
        functools.partial(_na_lat_attn_kernel, n_rows),
        grid=(bsz, NA_PAIRS, n_rows // rows_per_step),
        in_specs=[row_blk, seq_blk, seq_blk, ctx_blk, ctx_blk,
                  pl.BlockSpec((2, NA_BIAS_SLOTS, GRID_W, LANES), lambda b, j, r: (j, 0, 0, 0)),
                  row_blk],
        out_specs=row_blk,
        out_shape=jax.ShapeDtypeStruct((bsz, seq_len, NA_WIDTH), BF16),
        compiler_params=_params(3),
        name="na_lat_attention",
    )(q, k, v, k_ctx, v_ctx, bias, z)


def _rope_swap_perm():
    nf = QK_ROPE // 4
    idx = np.arange(QK_ROPE).reshape(2, 2, nf)
    return idx[:, ::-1, :].reshape(-1)


def _rope_table(seq_len):
    nf = QK_ROPE // 4
    inv = ROPE_BASE ** (-jnp.arange(nf, dtype=F32) / nf)
    t = jnp.arange(seq_len)
    pos = jnp.stack([t // GRID_W, t % GRID_W], -1).astype(F32)
    ang = pos[:, :, None] * inv
    cos = jnp.cos(ang)
    sin = jnp.sin(ang)
    c = jnp.concatenate([cos, cos], axis=-1).reshape(seq_len, QK_ROPE)
    s = jnp.concatenate([-sin, sin], axis=-1).reshape(seq_len, QK_ROPE)
    return jnp.concatenate([c, s], axis=-1)


def _mla_weights(w_in, w_uq):
    perm = _rope_swap_perm()
    o1, o2, o3 = Q_LORA, Q_LORA + KV_LORA, Q_LORA + KV_LORA + QK_ROPE
    w_kr = w_in[:, o2:o3]
    w1 = jnp.concatenate([w_in[:, :o2], w_kr, w_kr[:, perm], w_in[:, o3:]], axis=1).astype(BF16)
    wq = w_uq.reshape(Q_LORA, MLA_HEADS, QK_NOPE + QK_ROPE)
    nope = wq[:, :, :QK_NOPE].reshape(Q_LORA, MLA_HEADS * QK_NOPE)
    rope = wq[:, :, QK_NOPE:]
    rope_blk = jnp.concatenate([rope, rope[:, :, perm]], axis=-1).reshape(Q_LORA, MLA_HEADS * LANES)
    return w1, jnp.concatenate([nope, rope_blk], axis=1).astype(BF16)


def kernel(x_prompt, x_sample, cache_mla_ckv, cache_mla_krope, cache_na_k, cache_na_v, c, c_ctx, ada_w, ada_b, ln_g, ln_b, pool_w_in, pool_w_grp, pool_scale, pool_w_out, mla_w_in, mla_q_norm, mla_w_uq, mla_kv_norm, mla_w_ukv, mla_w_out, na_w_in, na_rpb, na_w_out):
    n_ctx, n_lat = x_prompt.shape[0], x_sample.shape[0]
    lat_len = x_sample.shape[1]
    cond = jnp.concatenate([c_ctx[None], c, jnp.zeros((8 - 1 - n_lat, D_MODEL), F32)], axis=0)
    mod = _ada(cond, ada_w, ada_b)
    yp, ys = x_prompt, x_sample
    st_ckv = st_kr = st_k = st_v = None
    for i in range(DEPTH):
        kind, j = i % N_MIXERS, i // N_MIXERS
        mod_p = jnp.broadcast_to(mod[i, 0:1][None], (n_ctx, 1, 3 * D_MODEL))
        mod_s = mod[i, 1:1 + n_lat][:, None, :]
        g, b = ln_g[i][None], ln_b[i][None]
        if kind == 0:
            w_u = pool_w_in[j][:, :D_MODEL].astype(BF16)
            w_z = pool_w_in[j][:, D_MODEL:].astype(BF16)
            w_g = pool_w_grp[j].astype(BF16)
            w_o = pool_w_out[j].astype(BF16)
            ps = pool_scale[j][None]
            yp = _pool_layer(yp, mod_p, w_u, w_z, w_g, ps, w_o, g, b)
            ys = _pool_layer(ys, mod_s, w_u, w_z, w_g, ps, w_o, g, b)
        elif kind == 1:
            w1, w_uq = _mla_weights(mla_w_in[j], mla_w_uq[j])
            w_ukv = mla_w_ukv[j].astype(BF16)
            w_o = mla_w_out[j].astype(BF16)
            qn, kvn = mla_q_norm[j][None], mla_kv_norm[j][None]
            q_p, kv_p, krp_p, z_p, st_ckv, st_kr = _mla_proj(yp, mod_p, w1, qn, kvn, w_uq, w_ukv, None, True)
            a_p = _mla_attention(q_p, z_p, [(kv_p, krp_p)], ROW_TILE, MLA_HEADS, ROW_TILE)
            yp = _out_layer(a_p, yp, mod_p, w_o, g, b)
            q_s, kv_s, krp_s, z_s = _mla_proj(ys, mod_s, w1, qn, kvn, w_uq, w_ukv, _rope_table(lat_len), False)
            kv_c, krp_c = _mla_cache_keys(cache_mla_ckv, cache_mla_krope, j, w_ukv)
            a_s = _mla_attention(q_s, z_s, [(kv_c, krp_c), (kv_s, krp_s)], lat_len, 1, 2 * ROW_TILE)
            ys = _out_layer(a_s, ys, mod_s, w_o, g, b)
        else:
            w_in = na_w_in[j].astype(BF16)
            w_o = na_w_out[j].astype(BF16)
            q_p, k_p, v_p, z_p, st_k, st_v = _na_proj(yp, mod_p, w_in, True)
            a_p = _na_ctx_attention(q_p, k_p, v_p, z_p)
            yp = _out_layer(a_p, yp, mod_p, w_o, g, b)
            q_s, k_s, v_s, z_s = _na_proj(ys, mod_s, w_in, False)
            past = cache_na_k.shape[2]
            k_c = cache_na_k[:, j].reshape(n_lat, past, NA_WIDTH).astype(BF16)
            v_c = cache_na_v[:, j].reshape(n_lat, past, NA_WIDTH).astype(BF16)
            bias = _na_bias_table(na_rpb[j])
            a_s = _na_lat_attention(q_s, k_s, v_s, k_c, v_c, bias, z_s)
            ys = _out_layer(a_s, ys, mod_s, w_o, g, b)
    state_na_k = st_k.reshape(n_ctx, 1, -1, NA_HEADS, NA_HEAD_DIM)
    state_na_v = st_v.reshape(n_ctx, 1, -1, NA_HEADS, NA_HEAD_DIM)
    return (yp, ys, st_ckv, st_kr, state_na_k, state_na_v)
```

```python
import functools
import math

import numpy as np
import jax
import jax.numpy as jnp
from jax import lax
from jax.experimental import pallas as pl
from jax.experimental.pallas import tpu as pltpu

F32 = jnp.float32
BF16 = jnp.bfloat16

D_MODEL = 1024
DEPTH = 4
GRID_W = 64
N_MIXERS = 3

POOL_WINDOWS = (2, 4, 8, 16)
POOL_GROUP_DIM = D_MODEL // len(POOL_WINDOWS)
POOL_HALO = 8

MLA_HEADS = 8
QK_NOPE = 128
QK_ROPE = 64
V_HEAD = 128
Q_LORA = D_MODEL // 2
KV_LORA = D_MODEL // 4
MLA_WIDTH = MLA_HEADS * V_HEAD
MLA_SCALE = (QK_NOPE + QK_ROPE) ** -0.5

NA_HEADS = 16
NA_HEAD_DIM = 64
NA_WIDTH = NA_HEADS * NA_HEAD_DIM
NA_WIN_R = 8
NA_WIN_C = 16
NA_SCALE = NA_HEAD_DIM ** -0.5
NA_PAIRS = NA_HEADS // 2

ROPE_BASE = 10000.0
LN_EPS = 1e-5
RMS_EPS = 1e-6
NEG_INF = -1e30
DEEPNORM_ALPHA = (2 * DEPTH) ** 0.25
LOG2E = math.log2(math.e)

LANES = 128
ROW_TILE = 512
ATTN_ROWS = 256
VMEM_LIMIT = 56 * 1024 * 1024


def _params(n_axes):
    return pltpu.CompilerParams(dimension_semantics=("arbitrary",) * n_axes, vmem_limit_bytes=VMEM_LIMIT)


def _silu(x):
    return x * jax.nn.sigmoid(x)


def _dot(a, b):
    return jnp.dot(a, b, preferred_element_type=F32)


def _dot_nt(a, b):
    return lax.dot_general(a, b, (((1,), (1,)), ((), ())), preferred_element_type=F32)


def _modulation(mod_ref):
    m = mod_ref[0]
    return m[:, :D_MODEL], m[:, D_MODEL:2 * D_MODEL], m[:, 2 * D_MODEL:]


def _deepnorm_ln(x, gate, branch, g, b):
    xf = DEEPNORM_ALPHA * x + gate * branch
    mu = jnp.mean(xf, axis=-1, keepdims=True)
    xc = xf - mu
    var = jnp.mean(xc * xc, axis=-1, keepdims=True)
    return xc * lax.rsqrt(var + LN_EPS) * g + b


def _rms(x, g):
    return x * lax.rsqrt(jnp.mean(x * x, axis=-1, keepdims=True) + RMS_EPS) * g


def _ada_kernel(cond_ref, w_ref, b_ref, o_ref):
    c = cond_ref[...]
    o_ref[0] = _dot(_silu(c).astype(BF16), w_ref[0].astype(BF16)) + b_ref[0]


def _ada(cond, ada_w, ada_b):
    tn = D_MODEL
    return pl.pallas_call(
        _ada_kernel,
        grid=(DEPTH, 3 * D_MODEL // tn),
        in_specs=[
            pl.BlockSpec((8, D_MODEL), lambda l, n: (0, 0)),
            pl.BlockSpec((1, D_MODEL, tn), lambda l, n: (l, 0, n)),
            pl.BlockSpec((1, 1, tn), lambda l, n: (l, 0, n)),
        ],
        out_specs=pl.BlockSpec((1, 8, tn), lambda l, n: (l, 0, n)),
        out_shape=jax.ShapeDtypeStruct((DEPTH, 8, 3 * D_MODEL), F32),
        compiler_params=_params(2),
        name="ada",
    )(cond, ada_w, ada_b.reshape(DEPTH, 1, 3 * D_MODEL))


def _rows_ahead(x, k):
    return x if k == 0 else pltpu.roll(x, x.shape[0] - k, 0)


def _pool_kernel(seq_len, x_ref, xp_ref, xn_ref, mod_ref, wu_ref, wz_ref, wg_ref, ps_ref, wo_ref,
                 lng_ref, lnb_ref, o_ref):
    t_rows = x_ref.shape[1]
    i = pl.program_id(1)
    n_tiles = pl.num_programs(1)
    shift, scale, gate = _modulation(mod_ref)
    one_scale = 1.0 + scale
    x = x_ref[0]
    hm = x * one_scale + shift
    hp = xp_ref[0] * one_scale + shift
    hn = xn_ref[0] * one_scale + shift
    hext = jnp.concatenate([hm, hp, hn], axis=0).astype(BF16)
    u = _dot(hext, wu_ref[...])
    z = _dot(hext[:t_rows], wz_ref[...])
    prev_ok = (i > 0).astype(F32)
    next_ok = (i < n_tiles - 1).astype(F32)
    centre = u[:t_rows]
    useq = jnp.concatenate([u[t_rows:t_rows + POOL_HALO] * prev_ok, centre, u[t_rows + POOL_HALO:] * next_ok], axis=0)

    t = i * t_rows + lax.broadcasted_iota(jnp.int32, (t_rows, 1), 0)
    mixed = []
    for g, w in enumerate(POOL_WINDOWS):
        cols = slice(g * POOL_GROUP_DIM, (g + 1) * POOL_GROUP_DIM)
        q = useq[:, cols]
        span = 1
        while span < w // 2:
            q = q + _rows_ahead(q, span)
            span *= 2
        acc = _rows_ahead(q, POOL_HALO - w // 2)[:t_rows] + q[POOL_HALO:POOL_HALO + t_rows]
        lo = jnp.clip(t - w // 2, 0, seq_len)
        hi = jnp.clip(t + (w - w // 2), 0, seq_len)
        inv_cnt = 1.0 / (hi - lo).astype(F32)
        mg = (acc * inv_cnt - centre[:, cols]).astype(BF16)
        mixed.append(_dot(mg, wg_ref[g]))
    mixed = jnp.concatenate(mixed, axis=1) * ps_ref[...]
    a = (mixed * _silu(z)).astype(BF16)
    branch = _dot(a, wo_ref[...])
    o_ref[0] = _deepnorm_ln(x, gate, branch, lng_ref[...], lnb_ref[...])


def _pool_layer(x, mod, w_u, w_z, w_grp, p_scale, w_out, ln_g, ln_b):
    bsz, seq_len, _ = x.shape
    t_rows = min(ROW_TILE, seq_len)
    n_tiles = seq_len // t_rows
    halo_per_tile = t_rows // POOL_HALO
    halo_per_seq = seq_len // POOL_HALO
    n_halo = bsz * halo_per_seq
    xh = x.reshape(n_halo, POOL_HALO, D_MODEL)
    const2 = lambda b, i: (0, 0)
    return pl.pallas_call(
        functools.partial(_pool_kernel, seq_len),
        grid=(bsz, n_tiles),
        in_specs=[
            pl.BlockSpec((1, t_rows, D_MODEL), lambda b, i: (b, i, 0)),
            pl.BlockSpec((1, POOL_HALO, D_MODEL),
                         lambda b, i: (jnp.maximum(b * halo_per_seq + i * halo_per_tile - 1, 0), 0, 0)),
            pl.BlockSpec((1, POOL_HALO, D_MODEL),
                         lambda b, i: (jnp.minimum(b * halo_per_seq + (i + 1) * halo_per_tile, n_halo - 1), 0, 0)),
            pl.BlockSpec((1, 1, 3 * D_MODEL), lambda b, i: (b, 0, 0)),
            pl.BlockSpec((D_MODEL, D_MODEL), const2),
            pl.BlockSpec((D_MODEL, D_MODEL), const2),
            pl.BlockSpec((len(POOL_WINDOWS), POOL_GROUP_DIM, POOL_GROUP_DIM), lambda b, i: (0, 0, 0)),
            pl.BlockSpec((1, D_MODEL), const2),
            pl.BlockSpec((D_MODEL, D_MODEL), const2),
            pl.BlockSpec((1, D_MODEL), const2),
            pl.BlockSpec((1, D_MODEL), const2),
        ],
        out_specs=pl.BlockSpec((1, t_rows, D_MODEL), lambda b, i: (b, i, 0)),
        out_shape=jax.ShapeDtypeStruct(x.shape, F32),
        compiler_params=_params(2),
        name="pool_layer",
    )(x, xh, xh, mod, w_u, w_z, w_grp, p_scale, w_out, ln_g, ln_b)


def _out_kernel(a_ref, x_ref, mod_ref, wo_ref, lng_ref, lnb_ref, o_ref):
    _, _, gate = _modulation(mod_ref)
    branch = _dot(a_ref[0], wo_ref[...])
    o_ref[0] = _deepnorm_ln(x_ref[0], gate, branch, lng_ref[...], lnb_ref[...])


def _out_layer(a, x, mod, w_out, ln_g, ln_b):
    bsz, seq_len, _ = x.shape
    t_rows = min(ROW_TILE, seq_len)
    const2 = lambda b, i: (0, 0)
    tok = pl.BlockSpec((1, t_rows, D_MODEL), lambda b, i: (b, i, 0))
    return pl.pallas_call(
        _out_kernel,
        grid=(bsz, seq_len // t_rows),
        in_specs=[tok, tok,
                  pl.BlockSpec((1, 1, 3 * D_MODEL), lambda b, i: (b, 0, 0)),
                  pl.BlockSpec((D_MODEL, D_MODEL), const2),
                  pl.BlockSpec((1, D_MODEL), const2),
                  pl.BlockSpec((1, D_MODEL), const2)],
        out_specs=tok,
        out_shape=jax.ShapeDtypeStruct(x.shape, F32),
        compiler_params=_params(2),
        name="out_layer",
    )(a, x, mod, w_out, ln_g, ln_b)


MLA_W1_COLS = Q_LORA + KV_LORA + 2 * QK_ROPE + MLA_WIDTH
MLA_Z_OFF = Q_LORA + KV_LORA + 2 * QK_ROPE
MLA_Q_COLS = MLA_HEADS * QK_NOPE + MLA_HEADS * LANES
MLA_Q_SCALE = MLA_SCALE * LOG2E


def _rope_block(blk, cs, low_half):
    if cs is not None:
        blk = blk * cs
        blk = blk + pltpu.roll(blk, LANES // 2, 1)
    return jnp.where(low_half, blk, 0.0)


def _mla_proj_kernel(rope, emit_state, *refs):
    x_ref, mod_ref, w1_ref, qn_ref, kvn_ref, wuq_ref, wukv_ref = refs[:7]
    refs = refs[7:]
    cs = None
    if rope:
        cs = refs[0][...]
        refs = refs[1:]
    q_ref, kv_ref, krp_ref, z_ref = refs[:4]
    shift, scale, _ = _modulation(mod_ref)
    h = (x_ref[0] * (1.0 + scale) + shift).astype(BF16)
    p = _dot(h, w1_ref[...])
    cq = p[:, :Q_LORA]
    ckv = p[:, Q_LORA:Q_LORA + KV_LORA]
    krb = p[:, Q_LORA + KV_LORA:MLA_Z_OFF]
    z_ref[0] = p[:, MLA_Z_OFF:]
    ckvn = _rms(ckv, kvn_ref[...])
    if emit_state:
        ckv_ref, kr_ref = refs[4:6]
        ckv_ref[0, 0] = ckvn
        kr_ref[0, 0] = krb[:, :QK_ROPE]
    low_half = lax.broadcasted_iota(jnp.int32, (1, LANES), 1) < LANES // 2
    krp_ref[0] = _rope_block(krb, cs, low_half).astype(BF16)
    kv_ref[0] = _dot(ckvn.astype(BF16), wukv_ref[...]).astype(BF16)
    qf = _dot(_rms(cq, qn_ref[...]).astype(BF16), wuq_ref[...])
    n_nope = MLA_HEADS * QK_NOPE
    q_ref[0, :, :n_nope] = (qf[:, :n_nope] * MLA_Q_SCALE).astype(BF16)
    for hh in range(MLA_HEADS):
        cols = slice(n_nope + hh * LANES, n_nope + (hh + 1) * LANES)
        q_ref[0, :, cols] = (_rope_block(qf[:, cols], cs, low_half) * MLA_Q_SCALE).astype(BF16)


def _mla_proj(x, mod, w1, q_norm, kv_norm, w_uq, w_ukv, rope_cs, emit_state):
    bsz, seq_len, _ = x.shape
    t_rows = min(ROW_TILE, seq_len)
    const2 = lambda b, i: (0, 0)
    tok = lambda width: pl.BlockSpec((1, t_rows, width), lambda b, i: (b, i, 0))
    in_specs = [tok(D_MODEL),
                pl.BlockSpec((1, 1, 3 * D_MODEL), lambda b, i: (b, 0, 0)),
                pl.BlockSpec((D_MODEL, MLA_W1_COLS), const2),
                pl.BlockSpec((1, Q_LORA), const2),
                pl.BlockSpec((1, KV_LORA), const2),
                pl.BlockSpec((Q_LORA, MLA_Q_COLS), const2),
                pl.BlockSpec((KV_LORA, MLA_HEADS * (QK_NOPE + V_HEAD)), const2)]
    args = [x, mod, w1, q_norm, kv_norm, w_uq, w_ukv]
    if rope_cs is not None:
        in_specs.append(pl.BlockSpec((t_rows, LANES), lambda b, i: (i, 0)))
        args.append(rope_cs)
    out_specs = [tok(MLA_Q_COLS), tok(MLA_HEADS * (QK_NOPE + V_HEAD)), tok(LANES), tok(MLA_WIDTH)]
    out_shape = [jax.ShapeDtypeStruct((bsz, seq_len, MLA_Q_COLS), BF16),
                 jax.ShapeDtypeStruct((bsz, seq_len, MLA_HEADS * (QK_NOPE + V_HEAD)), BF16),
                 jax.ShapeDtypeStruct((bsz, seq_len, LANES), BF16),
                 jax.ShapeDtypeStruct((bsz, seq_len, MLA_WIDTH), F32)]
    if emit_state:
        out_specs += [pl.BlockSpec((1, 1, t_rows, KV_LORA), lambda b, i: (b, 0, i, 0)),
                      pl.BlockSpec((1, 1, t_rows, QK_ROPE), lambda b, i: (b, 0, i, 0))]
        out_shape += [jax.ShapeDtypeStruct((bsz, 1, seq_len, KV_LORA), F32),
                      jax.ShapeDtypeStruct((bsz, 1, seq_len, QK_ROPE), F32)]
    return pl.pallas_call(
        functools.partial(_mla_proj_kernel, rope_cs is not None, emit_state),
        grid=(bsz, seq_len // t_rows),
        in_specs=in_specs,
        out_specs=out_specs,
        out_shape=out_shape,
        compiler_params=_params(2),
        name="mla_proj",
    )(*args)


def _mla_cache_kernel(ckv_ref, kr_ref, wukv_ref, kv_ref, krp_ref):
    kv_ref[0] = _dot(ckv_ref[0, 0].astype(BF16), wukv_ref[...]).astype(BF16)
    krp_ref[0, :, :QK_ROPE] = kr_ref[0, 0].astype(BF16)
    krp_ref[0, :, QK_ROPE:] = jnp.zeros((kr_ref.shape[2], LANES - QK_ROPE), BF16)


def _mla_cache_keys(cache_ckv, cache_kr, slot, w_ukv):
    bsz, _, past, _ = cache_ckv.shape
    kv_cols = MLA_HEADS * (QK_NOPE + V_HEAD)
    return pl.pallas_call(
        _mla_cache_kernel,
        grid=(bsz,),
        in_specs=[pl.BlockSpec((1, 1, past, KV_LORA), lambda b: (b, slot, 0, 0)),
                  pl.BlockSpec((1, 1, past, QK_ROPE), lambda b: (b, slot, 0, 0)),
                  pl.BlockSpec((KV_LORA, kv_cols), lambda b: (0, 0))],
        out_specs=[pl.BlockSpec((1, past, kv_cols), lambda b: (b, 0, 0)),
                   pl.BlockSpec((1, past, LANES), lambda b: (b, 0, 0))],
        out_shape=[jax.ShapeDtypeStruct((bsz, past, kv_cols), BF16),
                   jax.ShapeDtypeStruct((bsz, past, LANES), BF16)],
        compiler_params=_params(1),
        name="mla_cache_keys",
    )(cache_ckv, cache_kr, w_ukv)


def _softmax_pv(scores, values):
    m = functools.reduce(jnp.maximum, [jnp.max(s, axis=1, keepdims=True) for s in scores])
    denom = 0.0
    acc = 0.0
    for s, v in zip(scores, values):
        p = jnp.exp2(s - m)
        denom = denom + jnp.sum(p, axis=1, keepdims=True)
        acc = acc + _dot(p.astype(BF16), v)
    return acc / denom


def _mla_attn_kernel(n_seg, heads, sub_rows, qn_ref, qr_ref, z_ref, *refs):
    o_ref = refs[2 * n_seg]
    q_rows = qn_ref.shape[1]
    kv_w = QK_NOPE + V_HEAD
    for h in range(heads):
        lanes = slice(h * LANES, (h + 1) * LANES)
        keys, values = [], []
        for s in range(n_seg):
            kv_ref, kr_ref = refs[2 * s:2 * s + 2]
            keys.append(jnp.concatenate([kv_ref[0, :, h * kv_w:h * kv_w + QK_NOPE], kr_ref[0]], axis=1))
            values.append(kv_ref[0, :, h * kv_w + QK_NOPE:(h + 1) * kv_w])
        for r0 in range(0, q_rows, sub_rows):
            rows = slice(r0, r0 + sub_rows)
            q = jnp.concatenate([qn_ref[0, rows, lanes], qr_ref[0, rows, lanes]], axis=1)
            o = _softmax_pv([_dot_nt(q, k) for k in keys], values)
            o_ref[0, rows, lanes] = (o * _silu(z_ref[0, rows, lanes])).astype(BF16)


def _mla_attention(q, z, segments, q_tile, heads, sub_rows):
    bsz, lq, _ = q.shape
    n_hblk = MLA_HEADS // heads
    hw = heads * LANES
    in_specs = [pl.BlockSpec((1, q_tile, hw), lambda b, h, i: (b, i, h)),
                pl.BlockSpec((1, q_tile, hw), lambda b, h, i: (b, i, n_hblk + h)),
                pl.BlockSpec((1, q_tile, hw), lambda b, h, i: (b, i, h))]
    args = [q, q, z]
    for kv, krp in segments:
        lk = kv.shape[1]
        in_specs += [pl.BlockSpec((1, lk, 2 * hw), lambda b, h, i: (b, 0, h)),
                     pl.BlockSpec((1, lk, LANES), lambda b, h, i: (b, 0, 0))]
        args += [kv, krp]
    return pl.pallas_call(
        functools.partial(_mla_attn_kernel, len(segments), heads, sub_rows),
        grid=(bsz, n_hblk, lq // q_tile),
        in_specs=in_specs,
        out_specs=pl.BlockSpec((1, q_tile, hw), lambda b, h, i: (b, i, h)),
        out_shape=jax.ShapeDtypeStruct((bsz, lq, MLA_WIDTH), BF16),
        compiler_params=_params(3),
        name="mla_attention",
    )(*args)


NA_Q_SCALE = NA_SCALE * LOG2E


def _na_proj_kernel(emit_state, x_ref, mod_ref, w_ref, q_ref, k_ref, v_ref, z_ref, *state_refs):
    shift, scale, _ = _modulation(mod_ref)
    h = (x_ref[0] * (1.0 + scale) + shift).astype(BF16)
    w = NA_WIDTH
    q_ref[0] = (_dot(h, w_ref[:, :w]) * NA_Q_SCALE).astype(BF16)
    k = _dot(h, w_ref[:, w:2 * w])
    v = _dot(h, w_ref[:, 2 * w:3 * w])
    k_ref[0] = k.astype(BF16)
    v_ref[0] = v.astype(BF16)
    z_ref[0] = _dot(h, w_ref[:, 3 * w:])
    if emit_state:
        sk_ref, sv_ref = state_refs
        sk_ref[0, 0] = k
        sv_ref[0, 0] = v


def _na_proj(x, mod, w_in, emit_state):
    bsz, seq_len, _ = x.shape
    t_rows = min(ROW_TILE, seq_len)
    tok = pl.BlockSpec((1, t_rows, NA_WIDTH), lambda b, i: (b, i, 0))
    out_specs = [tok, tok, tok, tok]
    act = jax.ShapeDtypeStruct((bsz, seq_len, NA_WIDTH), BF16)
    out_shape = [act, act, act, jax.ShapeDtypeStruct((bsz, seq_len, NA_WIDTH), F32)]
    if emit_state:
        st = pl.BlockSpec((1, 1, t_rows, NA_WIDTH), lambda b, i: (b, 0, i, 0))
        out_specs += [st, st]
        out_shape += [jax.ShapeDtypeStruct((bsz, 1, seq_len, NA_WIDTH), F32)] * 2
    return pl.pallas_call(
        functools.partial(_na_proj_kernel, emit_state),
        grid=(bsz, seq_len // t_rows),
        in_specs=[pl.BlockSpec((1, t_rows, D_MODEL), lambda b, i: (b, i, 0)),
                  pl.BlockSpec((1, 1, 3 * D_MODEL), lambda b, i: (b, 0, 0)),
                  pl.BlockSpec((D_MODEL, 4 * NA_WIDTH), lambda b, i: (0, 0))],
        out_specs=out_specs,
        out_shape=out_shape,
        compiler_params=_params(2),
        name="na_proj",
    )(x, mod, w_in)


def _split_head_pair(q2, low_half):
    zero = jnp.zeros_like(q2)
    return jnp.concatenate([jnp.where(low_half, q2, zero), jnp.where(low_half, zero, q2)], axis=0)


def _na_ctx_attn_kernel(q_ref, k_ref, v_ref, z_ref, o_ref):
    n = q_ref.shape[1]
    low_half = lax.broadcasted_iota(jnp.int32, (1, LANES), 1) < NA_HEAD_DIM
    for j in range(NA_PAIRS):
        lanes = slice(j * LANES, (j + 1) * LANES)
        qs = _split_head_pair(q_ref[0, :, lanes], low_half)
        o = _softmax_pv([_dot_nt(qs, k_ref[0, :, lanes])], [v_ref[0, :, lanes]])
        o2 = jnp.where(low_half, o[:n], o[n:])
        o_ref[0, :, lanes] = (o2 * _silu(z_ref[0, :, lanes])).astype(BF16)


def _na_ctx_attention(q, k, v, z):
    bsz, seq_len, _ = q.shape
    blk = pl.BlockSpec((1, seq_len, NA_WIDTH), lambda b: (b, 0, 0))
    return pl.pallas_call(
        _na_ctx_attn_kernel,
        grid=(bsz,),
        in_specs=[blk, blk, blk, blk],
        out_specs=blk,
        out_shape=jax.ShapeDtypeStruct((bsz, seq_len, NA_WIDTH), BF16),
        compiler_params=_params(1),
        name="na_ctx_attention",
    )(q, k, v, z)


NA_N_DR = 2 * NA_WIN_R - 1
NA_BIAS_SLOTS = NA_N_DR + 1


def _na_bias_kernel(rpb_ref, o_ref):
    qc = lax.broadcasted_iota(jnp.int32, (GRID_W, LANES), 0)
    lane = lax.broadcasted_iota(jnp.int32, (GRID_W, LANES), 1)
    low_half = lane < GRID_W
    kc = jnp.where(low_half, lane, lane - GRID_W)
    cstart = jnp.clip(qc - NA_WIN_C // 2, 0, GRID_W - NA_WIN_C)
    valid = (kc >= cstart) & (kc < cstart + NA_WIN_C)
    for i in range(NA_BIAS_SLOTS):
        row_l = jnp.broadcast_to(rpb_ref[0, pl.ds(max(i - 1, 0), 1), :], (GRID_W, LANES))
        row_r = jnp.broadcast_to(rpb_ref[0, pl.ds(min(i, NA_N_DR - 1), 1), :], (GRID_W, LANES))
        left = pltpu.roll(row_l, LANES - (NA_WIN_C - 1), 1, stride=1, stride_axis=0)
        right = pltpu.roll(row_r, GRID_W - (NA_WIN_C - 1), 1, stride=1, stride_axis=0)
        tile = jnp.where(low_half, left, right)
        o_ref[0, i] = jnp.where(valid, tile * LOG2E, NEG_INF)


def _na_bias_table(rpb):
    n_dr, n_dc = rpb.shape[1:]
    rpb_rows = jnp.pad(rpb, ((0, 0), (0, 16 - n_dr), (0, LANES - n_dc)))
    return pl.pallas_call(
        _na_bias_kernel,
        grid=(NA_HEADS,),
        in_specs=[pl.BlockSpec((1, 16, LANES), lambda h: (h, 0, 0))],
        out_specs=pl.BlockSpec((1, NA_BIAS_SLOTS, GRID_W, LANES), lambda h: (h, 0, 0, 0)),
        out_shape=jax.ShapeDtypeStruct((NA_HEADS, NA_BIAS_SLOTS, GRID_W, LANES), F32),
        compiler_params=_params(1),
        name="na_bias_table",
    )(rpb_rows)


NA_ROWS_PER_GROUP = 4
NA_BAND_ROWS = NA_ROWS_PER_GROUP + NA_WIN_R
NA_GROUPS_PER_STEP = 4


def _na_lat_attn_kernel(n_rows, q_ref, k_ref, v_ref, kc_ref, vc_ref, b_ref, z_ref, o_ref):
    n_q = NA_ROWS_PER_GROUP * GRID_W
    low_half = lax.broadcasted_iota(jnp.int32, (1, LANES), 1) < NA_HEAD_DIM
    for grp in range(NA_GROUPS_PER_STEP):
        rows = slice(grp * n_q, (grp + 1) * n_q)
        r0 = (pl.program_id(2) * NA_GROUPS_PER_STEP + grp) * NA_ROWS_PER_GROUP
        band0 = jnp.clip(r0 - NA_WIN_R // 2, 0, n_rows - NA_BAND_ROWS)
        qs = _split_head_pair(q_ref[0, rows, :], low_half)
        key0 = pl.multiple_of(band0 * GRID_W, GRID_W)
        kl = k_ref[0, pl.ds(key0, NA_BAND_ROWS * GRID_W), :]
        vl = v_ref[0, pl.ds(key0, NA_BAND_ROWS * GRID_W), :]
        bias_rows = []
        for rr in range(NA_ROWS_PER_GROUP):
            r = r0 + rr
            win0 = jnp.clip(r - NA_WIN_R // 2, 0, n_rows - NA_WIN_R)
            blocks = []
            for cblk in range(NA_BAND_ROWS // 2):
                key_row = band0 + 2 * cblk
                slot = jnp.clip(key_row - r + NA_WIN_R, 0, NA_BIAS_SLOTS - 1)
                left_pen = jnp.where((key_row >= win0) & (key_row < win0 + NA_WIN_R), 0.0, NEG_INF)
                right_pen = jnp.where((key_row + 1 >= win0) & (key_row + 1 < win0 + NA_WIN_R), 0.0, NEG_INF)
                blocks.append((slot, jnp.where(low_half, left_pen, right_pen)))
            bias_rows.append(blocks)
        bias = jnp.concatenate(
            [jnp.concatenate(
                [jnp.concatenate([b_ref[e, slot] + pen for slot, pen in blocks], axis=1)
                 for blocks in bias_rows], axis=0)
             for e in range(2)], axis=0)
        o = _softmax_pv([_dot_nt(qs, kl) + bias, _dot_nt(qs, kc_ref[0])], [vl, vc_ref[0]])
        o2 = jnp.where(low_half, o[:n_q], o[n_q:])
        o_ref[0, rows, :] = (o2 * _silu(z_ref[0, rows, :])).astype(BF16)


def _na_lat_attention(q, k, v, k_ctx, v_ctx, bias, z):
    bsz, seq_len, _ = q.shape
    n_rows = seq_len // GRID_W
    past = k_ctx.shape[1]
    rows_per_step = NA_GROUPS_PER_STEP * NA_ROWS_PER_GROUP
    row_blk = pl.BlockSpec((1, rows_per_step * GRID_W, LANES), lambda b, j, r: (b, r, j))
    seq_blk = pl.BlockSpec((1, seq_len, LANES), lambda b, j, r: (b, 0, j))
    ctx_blk = pl.BlockSpec((1, past, LANES), lambda b, j, r: (b, 0, j))
    return pl.pallas_call(
        functools.partial(_na_lat_attn_kernel, n_rows),
        grid=(bsz, NA_PAIRS, n_rows // rows_per_step),
        in_specs=[row_blk, seq_blk, seq_blk, ctx_blk, ctx_blk,
                  pl.BlockSpec((2, NA_BIAS_SLOTS, GRID_W, LANES), lambda b, j, r: (j, 0, 0, 0)),
                  row_blk],
        out_specs=row_blk,
        out_shape=jax.ShapeDtypeStruct((bsz, seq_len, NA_WIDTH), BF16),
        compiler_params=_params(3),
        name="na_lat_attention",
    )(q, k, v, k_ctx, v_ctx, bias, z)


def _rope_swap_perm():
    nf = QK_ROPE // 4
    idx = np.arange(QK_ROPE).reshape(2, 2, nf)
    return idx[:, ::-1, :].reshape(-1)


def _rope_table(seq_len):
    nf = QK_ROPE // 4
    inv = ROPE_BASE ** (-jnp.arange(nf, dtype=F32) / nf)
    t = jnp.arange(seq_len)
    pos = jnp.stack([t // GRID_W, t % GRID_W], -1).astype(F32)
    ang = pos[:, :, None] * inv
    cos = jnp.cos(ang)
    sin = jnp.sin(ang)
    c = jnp.concatenate([cos, cos], axis=-1).reshape(seq_len, QK_ROPE)
    s = jnp.concatenate([-sin, sin], axis=-1).reshape(seq_len, QK_ROPE)
    return jnp.concatenate([c, s], axis=-1)


def _mla_weights(w_in, w_uq):
    perm = _rope_swap_perm()
    o1, o2, o3 = Q_LORA, Q_LORA + KV_LORA, Q_LORA + KV_LORA + QK_ROPE
    w_kr = w_in[:, o2:o3]
    w1 = jnp.concatenate([w_in[:, :o2], w_kr, w_kr[:, perm], w_in[:, o3:]], axis=1).astype(BF16)
    wq = w_uq.reshape(Q_LORA, MLA_HEADS, QK_NOPE + QK_ROPE)
    nope = wq[:, :, :QK_NOPE].reshape(Q_LORA, MLA_HEADS * QK_NOPE)
    rope = wq[:, :, QK_NOPE:]
    rope_blk = jnp.concatenate([rope, rope[:, :, perm]], axis=-1).reshape(Q_LORA, MLA_HEADS * LANES)
    return w1, jnp.concatenate([nope, rope_blk], axis=1).astype(BF16)


def kernel(x_prompt, x_sample, cache_mla_ckv, cache_mla_krope, cache_na_k, cache_na_v, c, c_ctx, ada_w, ada_b, ln_g, ln_b, pool_w_in, pool_w_grp, pool_scale, pool_w_out, mla_w_in, mla_q_norm, mla_w_uq, mla_kv_norm, mla_w_ukv, mla_w_out, na_w_in, na_rpb, na_w_out):
    n_ctx, n_lat = x_prompt.shape[0], x_sample.shape[0]
    lat_len = x_sample.shape[1]
    cond = jnp.concatenate([c_ctx[None], c, jnp.zeros((8 - 1 - n_lat, D_MODEL), F32)], axis=0)
    mod = _ada(cond, ada_w, ada_b)
    yp, ys = x_prompt, x_sample
    st_ckv = st_kr = st_k = st_v = None
    for i in range(DEPTH):
        kind, j = i % N_MIXERS, i // N_MIXERS
        mod_p = jnp.broadcast_to(mod[i, 0:1][None], (n_ctx, 1, 3 * D_MODEL))
        mod_s = mod[i, 1:1 + n_lat][:, None, :]
        g, b = ln_g[i][None], ln_b[i][None]
        if kind == 0:
            w_u = pool_w_in[j][:, :D_MODEL].astype(BF16)
            w_z = pool_w_in[j][:, D_MODEL:].astype(BF16)
            w_g = pool_w_grp[j].astype(BF16)
            w_o = pool_w_out[j].astype(BF16)
            ps = pool_scale[j][None]
            yp = _pool_layer(yp, mod_p, w_u, w_z, w_g, ps, w_o, g, b)
            ys = _pool_layer(ys, mod_s, w_u, w_z, w_g, ps, w_o, g, b)
        elif kind == 1:
            w1, w_uq = _mla_weights(mla_w_in[j], mla_w_uq[j])
            w_ukv = mla_w_ukv[j].astype(BF16)
            w_o = mla_w_out[j].astype(BF16)
            qn, kvn = mla_q_norm[j][None], mla_kv_norm[j][None]
            q_p, kv_p, krp_p, z_p, st_ckv, st_kr = _mla_proj(yp, mod_p, w1, qn, kvn, w_uq, w_ukv, None, True)
            a_p = _mla_attention(q_p, z_p, [(kv_p, krp_p)], yp.shape[1], MLA_HEADS, ATTN_ROWS)
            yp = _out_layer(a_p, yp, mod_p, w_o, g, b)
            q_s, kv_s, krp_s, z_s = _mla_proj(ys, mod_s, w1, qn, kvn, w_uq, w_ukv, _rope_table(lat_len), False)
            kv_c, krp_c = _mla_cache_keys(cache_mla_ckv, cache_mla_krope, j, w_ukv)
            a_s = _mla_attention(q_s, z_s, [(kv_c, krp_c), (kv_s, krp_s)], lat_len, 1, 2 * ATTN_ROWS)
            ys = _out_layer(a_s, ys, mod_s, w_o, g, b)
        else:
            w_in = na_w_in[j].astype(BF16)
            w_o = na_w_out[j].astype(BF16)
            q_p, k_p, v_p, z_p, st_k, st_v = _na_proj(yp, mod_p, w_in, True)
            a_p = _na_ctx_attention(q_p, k_p, v_p, z_p)
            yp = _out_layer(a_p, yp, mod_p, w_o, g, b)
            q_s, k_s, v_s, z_s = _na_proj(ys, mod_s, w_in, False)
            past = cache_na_k.shape[2]
            k_c = cache_na_k[:, j].reshape(n_lat, past, NA_WIDTH).astype(BF16)
            v_c = cache_na_v[:, j].reshape(n_lat, past, NA_WIDTH).astype(BF16)
            bias = _na_bias_table(na_rpb[j])
            a_s = _na_lat_attention(q_s, k_s, v_s, k_c, v_c, bias, z_s)
            ys = _out_layer(a_s, ys, mod_s, w_o, g, b)
    state_na_k = st_k.reshape(n_ctx, 1, -1, NA_HEADS, NA_HEAD_DIM)
    state_na_v = st_v.reshape(n_ctx, 1, -1, NA_HEADS, NA_HEAD_DIM)
    return (yp, ys, st_ckv, st_kr, state_na_k, state_na_v)
```

```python
import functools
import math

import numpy as np
import jax
import jax.numpy as jnp
from jax import lax
from jax.experimental import pallas as pl
from jax.experimental.pallas import tpu as pltpu

F32 = jnp.float32
BF16 = jnp.bfloat16

D_MODEL = 1024
DEPTH = 4
GRID_W = 64
N_MIXERS = 3

POOL_WINDOWS = (2, 4, 8, 16)
POOL_GROUP_DIM = D_MODEL // len(POOL_WINDOWS)
POOL_HALO = 8

MLA_HEADS = 8
QK_NOPE = 128
QK_ROPE = 64
V_HEAD = 128
Q_LORA = D_MODEL // 2
KV_LORA = D_MODEL // 4
MLA_WIDTH = MLA_HEADS * V_HEAD
MLA_SCALE = (QK_NOPE + QK_ROPE) ** -0.5

NA_HEADS = 16
NA_HEAD_DIM = 64
NA_WIDTH = NA_HEADS * NA_HEAD_DIM
NA_WIN_R = 8
NA_WIN_C = 16
NA_SCALE = NA_HEAD_DIM ** -0.5
NA_PAIRS = NA_HEADS // 2

ROPE_BASE = 10000.0
LN_EPS = 1e-5
RMS_EPS = 1e-6
NEG_INF = -1e30
DEEPNORM_ALPHA = (2 * DEPTH) ** 0.25
LOG2E = math.log2(math.e)

LANES = 128
ROW_TILE = 512
ATTN_ROWS = 256
VMEM_LIMIT = 56 * 1024 * 1024


def _params(n_axes):
    return pltpu.CompilerParams(dimension_semantics=("arbitrary",) * n_axes, vmem_limit_bytes=VMEM_LIMIT)


def _silu(x):
    h = 0.5 * x
    return h + h * jnp.tanh(h)


def _dot(a, b):
    return jnp.dot(a, b, preferred_element_type=F32)


def _dot_nt(a, b):
    return lax.dot_general(a, b, (((1,), (1,)), ((), ())), preferred_element_type=F32)


def _modulation(mod_ref):
    m = mod_ref[0]
    return m[:, :D_MODEL], m[:, D_MODEL:2 * D_MODEL], m[:, 2 * D_MODEL:]


def _deepnorm_ln(x, gate, branch, g, b):
    xf = DEEPNORM_ALPHA * x + gate * branch
    mu = jnp.mean(xf, axis=-1, keepdims=True)
    xc = xf - mu
    var = jnp.mean(xc * xc, axis=-1, keepdims=True)
    return xc * lax.rsqrt(var + LN_EPS) * g + b


def _rms(x, g):
    return x * lax.rsqrt(jnp.mean(x * x, axis=-1, keepdims=True) + RMS_EPS) * g


def _ada_kernel(cond_ref, w_ref, b_ref, o_ref):
    c = cond_ref[...]
    o_ref[0] = _dot(_silu(c).astype(BF16), w_ref[0].astype(BF16)) + b_ref[0]


def _ada(cond, ada_w, ada_b):
    tn = D_MODEL
    return pl.pallas_call(
        _ada_kernel,
        grid=(DEPTH, 3 * D_MODEL // tn),
        in_specs=[
            pl.BlockSpec((8, D_MODEL), lambda l, n: (0, 0)),
            pl.BlockSpec((1, D_MODEL, tn), lambda l, n: (l, 0, n)),
            pl.BlockSpec((1, 1, tn), lambda l, n: (l, 0, n)),
        ],
        out_specs=pl.BlockSpec((1, 8, tn), lambda l, n: (l, 0, n)),
        out_shape=jax.ShapeDtypeStruct((DEPTH, 8, 3 * D_MODEL), F32),
        compiler_params=_params(2),
        name="ada",
    )(cond, ada_w, ada_b.reshape(DEPTH, 1, 3 * D_MODEL))


def _rows_ahead(x, k):
    return x if k == 0 else pltpu.roll(x, x.shape[0] - k, 0)


def _pool_kernel(seq_len, x_ref, xp_ref, xn_ref, mod_ref, wu_ref, wz_ref, wg_ref, ps_ref, wo_ref,
                 lng_ref, lnb_ref, o_ref):
    t_rows = x_ref.shape[1]
    i = pl.program_id(1)
    n_tiles = pl.num_programs(1)
    shift, scale, gate = _modulation(mod_ref)
    one_scale = 1.0 + scale
    x = x_ref[0]
    hm = x * one_scale + shift
    hp = xp_ref[0] * one_scale + shift
    hn = xn_ref[0] * one_scale + shift
    hext = jnp.concatenate([hm, hp, hn], axis=0).astype(BF16)
    u = _dot(hext, wu_ref[...])
    z = _dot(hext[:t_rows], wz_ref[...])
    prev_ok = (i > 0).astype(F32)
    next_ok = (i < n_tiles - 1).astype(F32)
    centre = u[:t_rows]
    useq = jnp.concatenate([u[t_rows:t_rows + POOL_HALO] * prev_ok, centre, u[t_rows + POOL_HALO:] * next_ok], axis=0)

    t = i * t_rows + lax.broadcasted_iota(jnp.int32, (t_rows, 1), 0)
    mixed = []
    for g, w in enumerate(POOL_WINDOWS):
        cols = slice(g * POOL_GROUP_DIM, (g + 1) * POOL_GROUP_DIM)
        q = useq[:, cols]
        span = 1
        while span < w // 2:
            q = q + _rows_ahead(q, span)
            span *= 2
        acc = _rows_ahead(q, POOL_HALO - w // 2)[:t_rows] + q[POOL_HALO:POOL_HALO + t_rows]
        lo = jnp.clip(t - w // 2, 0, seq_len)
        hi = jnp.clip(t + (w - w // 2), 0, seq_len)
        inv_cnt = 1.0 / (hi - lo).astype(F32)
        mg = (acc * inv_cnt - centre[:, cols]).astype(BF16)
        mixed.append(_dot(mg, wg_ref[g]))
    mixed = jnp.concatenate(mixed, axis=1) * ps_ref[...]
    a = (mixed * _silu(z)).astype(BF16)
    branch = _dot(a, wo_ref[...])
    o_ref[0] = _deepnorm_ln(x, gate, branch, lng_ref[...], lnb_ref[...])


def _pool_layer(x, mod, w_u, w_z, w_grp, p_scale, w_out, ln_g, ln_b):
    bsz, seq_len, _ = x.shape
    t_rows = min(ROW_TILE, seq_len)
    n_tiles = seq_len // t_rows
    halo_per_tile = t_rows // POOL_HALO
    halo_per_seq = seq_len // POOL_HALO
    n_halo = bsz * halo_per_seq
    xh = x.reshape(n_halo, POOL_HALO, D_MODEL)
    const2 = lambda b, i: (0, 0)
    return pl.pallas_call(
        functools.partial(_pool_kernel, seq_len),
        grid=(bsz, n_tiles),
        in_specs=[
            pl.BlockSpec((1, t_rows, D_MODEL), lambda b, i: (b, i, 0)),
            pl.BlockSpec((1, POOL_HALO, D_MODEL),
                         lambda b, i: (jnp.maximum(b * halo_per_seq + i * halo_per_tile - 1, 0), 0, 0)),
            pl.BlockSpec((1, POOL_HALO, D_MODEL),
                         lambda b, i: (jnp.minimum(b * halo_per_seq + (i + 1) * halo_per_tile, n_halo - 1), 0, 0)),
            pl.BlockSpec((1, 1, 3 * D_MODEL), lambda b, i: (b, 0, 0)),
            pl.BlockSpec((D_MODEL, D_MODEL), const2),
            pl.BlockSpec((D_MODEL, D_MODEL), const2),
            pl.BlockSpec((len(POOL_WINDOWS), POOL_GROUP_DIM, POOL_GROUP_DIM), lambda b, i: (0, 0, 0)),
            pl.BlockSpec((1, D_MODEL), const2),
            pl.BlockSpec((D_MODEL, D_MODEL), const2),
            pl.BlockSpec((1, D_MODEL), const2),
            pl.BlockSpec((1, D_MODEL), const2),
        ],
        out_specs=pl.BlockSpec((1, t_rows, D_MODEL), lambda b, i: (b, i, 0)),
        out_shape=jax.ShapeDtypeStruct(x.shape, F32),
        compiler_params=_params(2),
        name="pool_layer",
    )(x, xh, xh, mod, w_u, w_z, w_grp, p_scale, w_out, ln_g, ln_b)


def _out_kernel(a_ref, x_ref, mod_ref, wo_ref, lng_ref, lnb_ref, o_ref):
    _, _, gate = _modulation(mod_ref)
    branch = _dot(a_ref[0], wo_ref[...])
    o_ref[0] = _deepnorm_ln(x_ref[0], gate, branch, lng_ref[...], lnb_ref[...])


def _out_layer(a, x, mod, w_out, ln_g, ln_b):
    bsz, seq_len, _ = x.shape
    t_rows = min(ROW_TILE, seq_len)
    const2 = lambda b, i: (0, 0)
    tok = pl.BlockSpec((1, t_rows, D_MODEL), lambda b, i: (b, i, 0))
    return pl.pallas_call(
        _out_kernel,
        grid=(bsz, seq_len // t_rows),
        in_specs=[tok, tok,
                  pl.BlockSpec((1, 1, 3 * D_MODEL), lambda b, i: (b, 0, 0)),
                  pl.BlockSpec((D_MODEL, D_MODEL), const2),
                  pl.BlockSpec((1, D_MODEL), const2),
                  pl.BlockSpec((1, D_MODEL), const2)],
        out_specs=tok,
        out_shape=jax.ShapeDtypeStruct(x.shape, F32),
        compiler_params=_params(2),
        name="out_layer",
    )(a, x, mod, w_out, ln_g, ln_b)


MLA_W1_COLS = Q_LORA + KV_LORA + 2 * QK_ROPE + MLA_WIDTH
MLA_Z_OFF = Q_LORA + KV_LORA + 2 * QK_ROPE
MLA_Q_COLS = MLA_HEADS * QK_NOPE + MLA_HEADS * LANES
MLA_Q_SCALE = MLA_SCALE * LOG2E


def _rope_block(blk, cs, low_half):
    if cs is not None:
        blk = blk * cs
        blk = blk + pltpu.roll(blk, LANES // 2, 1)
    return jnp.where(low_half, blk, 0.0)


def _mla_proj_kernel(rope, emit_state, *refs):
    x_ref, mod_ref, w1_ref, qn_ref, kvn_ref, wuq_ref, wukv_ref = refs[:7]
    refs = refs[7:]
    cs = None
    if rope:
        cs = refs[0][...]
        refs = refs[1:]
    q_ref, kv_ref, krp_ref, z_ref = refs[:4]
    shift, scale, _ = _modulation(mod_ref)
    h = (x_ref[0] * (1.0 + scale) + shift).astype(BF16)
    p = _dot(h, w1_ref[...])
    cq = p[:, :Q_LORA]
    ckv = p[:, Q_LORA:Q_LORA + KV_LORA]
    krb = p[:, Q_LORA + KV_LORA:MLA_Z_OFF]
    z_ref[0] = p[:, MLA_Z_OFF:]
    ckvn = _rms(ckv, kvn_ref[...])
    if emit_state:
        ckv_ref, kr_ref = refs[4:6]
        ckv_ref[0, 0] = ckvn
        kr_ref[0, 0] = krb[:, :QK_ROPE]
    low_half = lax.broadcasted_iota(jnp.int32, (1, LANES), 1) < LANES // 2
    krp_ref[0] = _rope_block(krb, cs, low_half).astype(BF16)
    kv_ref[0] = _dot(ckvn.astype(BF16), wukv_ref[...]).astype(BF16)
    qf = _dot(_rms(cq, qn_ref[...]).astype(BF16), wuq_ref[...])
    n_nope = MLA_HEADS * QK_NOPE
    q_ref[0, :, :n_nope] = (qf[:, :n_nope] * MLA_Q_SCALE).astype(BF16)
    for hh in range(MLA_HEADS):
        cols = slice(n_nope + hh * LANES, n_nope + (hh + 1) * LANES)
        q_ref[0, :, cols] = (_rope_block(qf[:, cols], cs, low_half) * MLA_Q_SCALE).astype(BF16)


def _mla_proj(x, mod, w1, q_norm, kv_norm, w_uq, w_ukv, rope_cs, emit_state):
    bsz, seq_len, _ = x.shape
    t_rows = min(ROW_TILE, seq_len)
    const2 = lambda b, i: (0, 0)
    tok = lambda width: pl.BlockSpec((1, t_rows, width), lambda b, i: (b, i, 0))
    in_specs = [tok(D_MODEL),
                pl.BlockSpec((1, 1, 3 * D_MODEL), lambda b, i: (b, 0, 0)),
                pl.BlockSpec((D_MODEL, MLA_W1_COLS), const2),
                pl.BlockSpec((1, Q_LORA), const2),
                pl.BlockSpec((1, KV_LORA), const2),
                pl.BlockSpec((Q_LORA, MLA_Q_COLS), const2),
                pl.BlockSpec((KV_LORA, MLA_HEADS * (QK_NOPE + V_HEAD)), const2)]
    args = [x, mod, w1, q_norm, kv_norm, w_uq, w_ukv]
    if rope_cs is not None:
        in_specs.append(pl.BlockSpec((t_rows, LANES), lambda b, i: (i, 0)))
        args.append(rope_cs)
    out_specs = [tok(MLA_Q_COLS), tok(MLA_HEADS * (QK_NOPE + V_HEAD)), tok(LANES), tok(MLA_WIDTH)]
    out_shape = [jax.ShapeDtypeStruct((bsz, seq_len, MLA_Q_COLS), BF16),
                 jax.ShapeDtypeStruct((bsz, seq_len, MLA_HEADS * (QK_NOPE + V_HEAD)), BF16),
                 jax.ShapeDtypeStruct((bsz, seq_len, LANES), BF16),
                 jax.ShapeDtypeStruct((bsz, seq_len, MLA_WIDTH), F32)]
    if emit_state:
        out_specs += [pl.BlockSpec((1, 1, t_rows, KV_LORA), lambda b, i: (b, 0, i, 0)),
                      pl.BlockSpec((1, 1, t_rows, QK_ROPE), lambda b, i: (b, 0, i, 0))]
        out_shape += [jax.ShapeDtypeStruct((bsz, 1, seq_len, KV_LORA), F32),
                      jax.ShapeDtypeStruct((bsz, 1, seq_len, QK_ROPE), F32)]
    return pl.pallas_call(
        functools.partial(_mla_proj_kernel, rope_cs is not None, emit_state),
        grid=(bsz, seq_len // t_rows),
        in_specs=in_specs,
        out_specs=out_specs,
        out_shape=out_shape,
        compiler_params=_params(2),
        name="mla_proj",
    )(*args)


def _mla_cache_kernel(ckv_ref, kr_ref, wukv_ref, kv_ref, krp_ref):
    kv_ref[0] = _dot(ckv_ref[0, 0].astype(BF16), wukv_ref[...]).astype(BF16)
    krp_ref[0, :, :QK_ROPE] = kr_ref[0, 0].astype(BF16)
    krp_ref[0, :, QK_ROPE:] = jnp.zeros((kr_ref.shape[2], LANES - QK_ROPE), BF16)


def _mla_cache_keys(cache_ckv, cache_kr, slot, w_ukv):
    bsz, _, past, _ = cache_ckv.shape
    kv_cols = MLA_HEADS * (QK_NOPE + V_HEAD)
    return pl.pallas_call(
        _mla_cache_kernel,
        grid=(bsz,),
        in_specs=[pl.BlockSpec((1, 1, past, KV_LORA), lambda b: (b, slot, 0, 0)),
                  pl.BlockSpec((1, 1, past, QK_ROPE), lambda b: (b, slot, 0, 0)),
                  pl.BlockSpec((KV_LORA, kv_cols), lambda b: (0, 0))],
        out_specs=[pl.BlockSpec((1, past, kv_cols), lambda b: (b, 0, 0)),
                   pl.BlockSpec((1, past, LANES), lambda b: (b, 0, 0))],
        out_shape=[jax.ShapeDtypeStruct((bsz, past, kv_cols), BF16),
                   jax.ShapeDtypeStruct((bsz, past, LANES), BF16)],
        compiler_params=_params(1),
        name="mla_cache_keys",
    )(cache_ckv, cache_kr, w_ukv)


def _softmax_pv(scores, values):
    m = functools.reduce(jnp.maximum, [jnp.max(s, axis=1, keepdims=True) for s in scores])
    denom = 0.0
    acc = 0.0
    for s, v in zip(scores, values):
        p = jnp.exp2(s - m)
        denom = denom + jnp.sum(p, axis=1, keepdims=True)
        acc = acc + _dot(p.astype(BF16), v)
    return acc / denom


def _mla_attn_kernel(n_seg, heads, sub_rows, qn_ref, qr_ref, z_ref, *refs):
    o_ref = refs[2 * n_seg]
    q_rows = qn_ref.shape[1]
    kv_w = QK_NOPE + V_HEAD
    for h in range(heads):
        lanes = slice(h * LANES, (h + 1) * LANES)
        keys, values = [], []
        for s in range(n_seg):
            kv_ref, kr_ref = refs[2 * s:2 * s + 2]
            keys.append(jnp.concatenate([kv_ref[0, :, h * kv_w:h * kv_w + QK_NOPE], kr_ref[0]], axis=1))
            values.append(kv_ref[0, :, h * kv_w + QK_NOPE:(h + 1) * kv_w])
        for r0 in range(0, q_rows, sub_rows):
            rows = slice(r0, r0 + sub_rows)
            q = jnp.concatenate([qn_ref[0, rows, lanes], qr_ref[0, rows, lanes]], axis=1)
            o = _softmax_pv([_dot_nt(q, k) for k in keys], values)
            o_ref[0, rows, lanes] = (o * _silu(z_ref[0, rows, lanes])).astype(BF16)


def _mla_attention(q, z, segments, q_tile, heads, sub_rows):
    bsz, lq, _ = q.shape
    n_hblk = MLA_HEADS // heads
    hw = heads * LANES
    in_specs = [pl.BlockSpec((1, q_tile, hw), lambda b, h, i: (b, i, h)),
                pl.BlockSpec((1, q_tile, hw), lambda b, h, i: (b, i, n_hblk + h)),
                pl.BlockSpec((1, q_tile, hw), lambda b, h, i: (b, i, h))]
    args = [q, q, z]
    for kv, krp in segments:
        lk = kv.shape[1]
        in_specs += [pl.BlockSpec((1, lk, 2 * hw), lambda b, h, i: (b, 0, h)),
                     pl.BlockSpec((1, lk, LANES), lambda b, h, i: (b, 0, 0))]
        args += [kv, krp]
    return pl.pallas_call(
        functools.partial(_mla_attn_kernel, len(segments), heads, sub_rows),
        grid=(bsz, n_hblk, lq // q_tile),
        in_specs=in_specs,
        out_specs=pl.BlockSpec((1, q_tile, hw), lambda b, h, i: (b, i, h)),
        out_shape=jax.ShapeDtypeStruct((bsz, lq, MLA_WIDTH), BF16),
        compiler_params=_params(3),
        name="mla_attention",
    )(*args)


def _out_epilogue(a_ref, x_ref, mod_ref, wo_ref, lng_ref, lnb_ref, y_ref):
    _, _, gate = _modulation(mod_ref)
    branch = _dot(a_ref[0], wo_ref[...])
    y_ref[0] = _deepnorm_ln(x_ref[0], gate, branch, lng_ref[...], lnb_ref[...])


def _mla_ctx_kernel(qn_ref, qr_ref, z_ref, kv_ref, kr_ref, x_ref, mod_ref, wo_ref, lng_ref, lnb_ref, y_ref, a_ref):
    _mla_attn_kernel(1, MLA_HEADS, ATTN_ROWS, qn_ref, qr_ref, z_ref, kv_ref, kr_ref, a_ref)
    _out_epilogue(a_ref, x_ref, mod_ref, wo_ref, lng_ref, lnb_ref, y_ref)


def _ctx_fused_specs(seq_len):
    const2 = lambda b: (0, 0)
    tok = pl.BlockSpec((1, seq_len, D_MODEL), lambda b: (b, 0, 0))
    return tok, [tok,
                 pl.BlockSpec((1, 1, 3 * D_MODEL), lambda b: (b, 0, 0)),
                 pl.BlockSpec((D_MODEL, D_MODEL), const2),
                 pl.BlockSpec((1, D_MODEL), const2),
                 pl.BlockSpec((1, D_MODEL), const2)]


def _mla_ctx_layer(q, z, kv, krp, x, mod, w_out, ln_g, ln_b):
    bsz, seq_len, _ = x.shape
    n_nope = MLA_HEADS * QK_NOPE
    tok, tail_specs = _ctx_fused_specs(seq_len)
    return pl.pallas_call(
        _mla_ctx_kernel,
        grid=(bsz,),
        in_specs=[pl.BlockSpec((1, seq_len, n_nope), lambda b: (b, 0, 0)),
                  pl.BlockSpec((1, seq_len, n_nope), lambda b: (b, 0, 1)),
                  tok,
                  pl.BlockSpec((1, seq_len, kv.shape[2]), lambda b: (b, 0, 0)),
                  pl.BlockSpec((1, seq_len, LANES), lambda b: (b, 0, 0))] + tail_specs,
        out_specs=tok,
        out_shape=jax.ShapeDtypeStruct(x.shape, F32),
        scratch_shapes=[pltpu.VMEM((1, seq_len, MLA_WIDTH), BF16)],
        compiler_params=_params(1),
        name="mla_ctx_layer",
    )(q, q, z, kv, krp, x, mod, w_out, ln_g, ln_b)


NA_Q_SCALE = NA_SCALE * LOG2E


def _na_proj_kernel(emit_state, x_ref, mod_ref, w_ref, q_ref, k_ref, v_ref, z_ref, *state_refs):
    shift, scale, _ = _modulation(mod_ref)
    h = (x_ref[0] * (1.0 + scale) + shift).astype(BF16)
    w = NA_WIDTH
    q_ref[0] = (_dot(h, w_ref[:, :w]) * NA_Q_SCALE).astype(BF16)
    k = _dot(h, w_ref[:, w:2 * w])
    v = _dot(h, w_ref[:, 2 * w:3 * w])
    k_ref[0] = k.astype(BF16)
    v_ref[0] = v.astype(BF16)
    z_ref[0] = _dot(h, w_ref[:, 3 * w:])
    if emit_state:
        sk_ref, sv_ref = state_refs
        sk_ref[0, 0] = k
        sv_ref[0, 0] = v


def _na_proj(x, mod, w_in, emit_state):
    bsz, seq_len, _ = x.shape
    t_rows = min(ROW_TILE, seq_len)
    tok = pl.BlockSpec((1, t_rows, NA_WIDTH), lambda b, i: (b, i, 0))
    out_specs = [tok, tok, tok, tok]
    act = jax.ShapeDtypeStruct((bsz, seq_len, NA_WIDTH), BF16)
    out_shape = [act, act, act, jax.ShapeDtypeStruct((bsz, seq_len, NA_WIDTH), F32)]
    if emit_state:
        st = pl.BlockSpec((1, 1, t_rows, NA_WIDTH), lambda b, i: (b, 0, i, 0))
        out_specs += [st, st]
        out_shape += [jax.ShapeDtypeStruct((bsz, 1, seq_len, NA_WIDTH), F32)] * 2
    return pl.pallas_call(
        functools.partial(_na_proj_kernel, emit_state),
        grid=(bsz, seq_len // t_rows),
        in_specs=[pl.BlockSpec((1, t_rows, D_MODEL), lambda b, i: (b, i, 0)),
                  pl.BlockSpec((1, 1, 3 * D_MODEL), lambda b, i: (b, 0, 0)),
                  pl.BlockSpec((D_MODEL, 4 * NA_WIDTH), lambda b, i: (0, 0))],
        out_specs=out_specs,
        out_shape=out_shape,
        compiler_params=_params(2),
        name="na_proj",
    )(x, mod, w_in)


def _split_head_pair(q2, low_half):
    zero = jnp.zeros_like(q2)
    return jnp.concatenate([jnp.where(low_half, q2, zero), jnp.where(low_half, zero, q2)], axis=0)


def _na_ctx_attn_kernel(q_ref, k_ref, v_ref, z_ref, o_ref):
    n = q_ref.shape[1]
    low_half = lax.broadcasted_iota(jnp.int32, (1, LANES), 1) < NA_HEAD_DIM
    for j in range(NA_PAIRS):
        lanes = slice(j * LANES, (j + 1) * LANES)
        qs = _split_head_pair(q_ref[0, :, lanes], low_half)
        o = _softmax_pv([_dot_nt(qs, k_ref[0, :, lanes])], [v_ref[0, :, lanes]])
        o2 = jnp.where(low_half, o[:n], o[n:])
        o_ref[0, :, lanes] = (o2 * _silu(z_ref[0, :, lanes])).astype(BF16)


def _na_ctx_kernel(q_ref, k_ref, v_ref, z_ref, x_ref, mod_ref, wo_ref, lng_ref, lnb_ref, y_ref, a_ref):
    _na_ctx_attn_kernel(q_ref, k_ref, v_ref, z_ref, a_ref)
    _out_epilogue(a_ref, x_ref, mod_ref, wo_ref, lng_ref, lnb_ref, y_ref)


def _na_ctx_layer(q, k, v, z, x, mod, w_out, ln_g, ln_b):
    bsz, seq_len, _ = x.shape
    tok, tail_specs = _ctx_fused_specs(seq_len)
    return pl.pallas_call(
        _na_ctx_kernel,
        grid=(bsz,),
        in_specs=[tok, tok, tok, tok] + tail_specs,
        out_specs=tok,
        out_shape=jax.ShapeDtypeStruct(x.shape, F32),
        scratch_shapes=[pltpu.VMEM((1, seq_len, NA_WIDTH), BF16)],
        compiler_params=_params(1),
        name="na_ctx_layer",
    )(q, k, v, z, x, mod, w_out, ln_g, ln_b)


NA_N_DR = 2 * NA_WIN_R - 1
NA_BIAS_SLOTS = NA_N_DR + 1


def _na_bias_kernel(rpb_ref, o_ref):
    qc = lax.broadcasted_iota(jnp.int32, (GRID_W, LANES), 0)
    lane = lax.broadcasted_iota(jnp.int32, (GRID_W, LANES), 1)
    low_half = lane < GRID_W
    kc = jnp.where(low_half, lane, lane - GRID_W)
    cstart = jnp.clip(qc - NA_WIN_C // 2, 0, GRID_W - NA_WIN_C)
    valid = (kc >= cstart) & (kc < cstart + NA_WIN_C)
    for i in range(NA_BIAS_SLOTS):
        row_l = jnp.broadcast_to(rpb_ref[0, pl.ds(max(i - 1, 0), 1), :], (GRID_W, LANES))
        row_r = jnp.broadcast_to(rpb_ref[0, pl.ds(min(i, NA_N_DR - 1), 1), :], (GRID_W, LANES))
        left = pltpu.roll(row_l, LANES - (NA_WIN_C - 1), 1, stride=1, stride_axis=0)
        right = pltpu.roll(row_r, GRID_W - (NA_WIN_C - 1), 1, stride=1, stride_axis=0)
        tile = jnp.where(low_half, left, right)
        o_ref[0, i] = jnp.where(valid, tile * LOG2E, NEG_INF)


def _na_bias_table(rpb):
    n_dr, n_dc = rpb.shape[1:]
    rpb_rows = jnp.pad(rpb, ((0, 0), (0, 16 - n_dr), (0, LANES - n_dc)))
    return pl.pallas_call(
        _na_bias_kernel,
        grid=(NA_HEADS,),
        in_specs=[pl.BlockSpec((1, 16, LANES), lambda h: (h, 0, 0))],
        out_specs=pl.BlockSpec((1, NA_BIAS_SLOTS, GRID_W, LANES), lambda h: (h, 0, 0, 0)),
        out_shape=jax.ShapeDtypeStruct((NA_HEADS, NA_BIAS_SLOTS, GRID_W, LANES), F32),
        compiler_params=_params(1),
        name="na_bias_table",
    )(rpb_rows)


NA_ROWS_PER_GROUP = 4
NA_BAND_ROWS = NA_ROWS_PER_GROUP + NA_WIN_R
NA_GROUPS_PER_STEP = 8


def _na_lat_attn_kernel(n_rows, n_groups, q_ref, k_ref, v_ref, kc_ref, vc_ref, b_ref, z_ref, o_ref):
    n_q = NA_ROWS_PER_GROUP * GRID_W
    low_half = lax.broadcasted_iota(jnp.int32, (1, LANES), 1) < NA_HEAD_DIM
    for grp in range(n_groups):
        rows = slice(grp * n_q, (grp + 1) * n_q)
        r0 = (pl.program_id(2) * n_groups + grp) * NA_ROWS_PER_GROUP
        band0 = jnp.clip(r0 - NA_WIN_R // 2, 0, n_rows - NA_BAND_ROWS)
        qs = _split_head_pair(q_ref[0, rows, :], low_half)
        key0 = pl.multiple_of(band0 * GRID_W, GRID_W)
        kl = k_ref[0, pl.ds(key0, NA_BAND_ROWS * GRID_W), :]
        vl = v_ref[0, pl.ds(key0, NA_BAND_ROWS * GRID_W), :]
        bias_rows = []
        for rr in range(NA_ROWS_PER_GROUP):
            r = r0 + rr
            win0 = jnp.clip(r - NA_WIN_R // 2, 0, n_rows - NA_WIN_R)
            blocks = []
            for cblk in range(NA_BAND_ROWS // 2):
                key_row = band0 + 2 * cblk
                slot = jnp.clip(key_row - r + NA_WIN_R, 0, NA_BIAS_SLOTS - 1)
                left_pen = jnp.where((key_row >= win0) & (key_row < win0 + NA_WIN_R), 0.0, NEG_INF)
                right_pen = jnp.where((key_row + 1 >= win0) & (key_row + 1 < win0 + NA_WIN_R), 0.0, NEG_INF)
                blocks.append((slot, jnp.where(low_half, left_pen, right_pen)))
            bias_rows.append(blocks)
        bias = jnp.concatenate(
            [jnp.concatenate(
                [jnp.concatenate([b_ref[e, slot] + pen for slot, pen in blocks], axis=1)
                 for blocks in bias_rows], axis=0)
             for e in range(2)], axis=0)
        o = _softmax_pv([_dot_nt(qs, kl) + bias, _dot_nt(qs, kc_ref[0])], [vl, vc_ref[0]])
        o2 = jnp.where(low_half, o[:n_q], o[n_q:])
        o_ref[0, rows, :] = (o2 * _silu(z_ref[0, rows, :])).astype(BF16)


def _na_lat_attention(q, k, v, k_ctx, v_ctx, bias, z):
    bsz, seq_len, _ = q.shape
    n_rows = seq_len // GRID_W
    past = k_ctx.shape[1]
    n_groups = min(NA_GROUPS_PER_STEP, n_rows // NA_ROWS_PER_GROUP)
    rows_per_step = n_groups * NA_ROWS_PER_GROUP
    row_blk = pl.BlockSpec((1, rows_per_step * GRID_W, LANES), lambda b, j, r: (b, r, j))
    seq_blk = pl.BlockSpec((1, seq_len, LANES), lambda b, j, r: (b, 0, j))
    ctx_blk = pl.BlockSpec((1, past, LANES), lambda b, j, r: (b, 0, j))
    return pl.pallas_call(
        functools.partial(_na_lat_attn_kernel, n_rows, n_groups),
        grid=(bsz, NA_PAIRS, n_rows // rows_per_step),
        in_specs=[row_blk, seq_blk, seq_blk, ctx_blk, ctx_blk,
                  pl.BlockSpec((2, NA_BIAS_SLOTS, GRID_W, LANES), lambda b, j, r: (j, 0, 0, 0)),
                  row_blk],
        out_specs=row_blk,
        out_shape=jax.ShapeDtypeStruct((bsz, seq_len, NA_WIDTH), BF16),
        compiler_params=_params(3),
        name="na_lat_attention",
    )(q, k, v, k_ctx, v_ctx, bias, z)


def _rope_swap_perm():
    nf = QK_ROPE // 4
    idx = np.arange(QK_ROPE).reshape(2, 2, nf)
    return idx[:, ::-1, :].reshape(-1)


def _rope_table(seq_len):
    nf = QK_ROPE // 4
    inv = ROPE_BASE ** (-jnp.arange(nf, dtype=F32) / nf)
    t = jnp.arange(seq_len)
    pos = jnp.stack([t // GRID_W, t % GRID_W], -1).astype(F32)
    ang = pos[:, :, None] * inv
    cos = jnp.cos(ang)
    sin = jnp.sin(ang)
    c = jnp.concatenate([cos, cos], axis=-1).reshape(seq_len, QK_ROPE)
    s = jnp.concatenate([-sin, sin], axis=-1).reshape(seq_len, QK_ROPE)
    return jnp.concatenate([c, s], axis=-1)


def _mla_weights(w_in, w_uq):
    perm = _rope_swap_perm()
    o1, o2, o3 = Q_LORA, Q_LORA + KV_LORA, Q_LORA + KV_LORA + QK_ROPE
    w_kr = w_in[:, o2:o3]
    w1 = jnp.concatenate([w_in[:, :o2], w_kr, w_kr[:, perm], w_in[:, o3:]], axis=1).astype(BF16)
    wq = w_uq.reshape(Q_LORA, MLA_HEADS, QK_NOPE + QK_ROPE)
    nope = wq[:, :, :QK_NOPE].reshape(Q_LORA, MLA_HEADS * QK_NOPE)
    rope = wq[:, :, QK_NOPE:]
    rope_blk = jnp.concatenate([rope, rope[:, :, perm]], axis=-1).reshape(Q_LORA, MLA_HEADS * LANES)
    return w1, jnp.concatenate([nope, rope_blk], axis=1).astype(BF16)


def kernel(x_prompt, x_sample, cache_mla_ckv, cache_mla_krope, cache_na_k, cache_na_v, c, c_ctx, ada_w, ada_b, ln_g, ln_b, pool_w_in, pool_w_grp, pool_scale, pool_w_out, mla_w_in, mla_q_norm, mla_w_uq, mla_kv_norm, mla_w_ukv, mla_w_out, na_w_in, na_rpb, na_w_out):
    n_ctx, n_lat = x_prompt.shape[0], x_sample.shape[0]
    lat_len = x_sample.shape[1]
    cond = jnp.concatenate([c_ctx[None], c, jnp.zeros((8 - 1 - n_lat, D_MODEL), F32)], axis=0)
    mod = _ada(cond, ada_w, ada_b)
    yp, ys = x_prompt, x_sample
    st_ckv = st_kr = st_k = st_v = None
    for i in range(DEPTH):
        kind, j = i % N_MIXERS, i // N_MIXERS
        mod_p = jnp.broadcast_to(mod[i, 0:1][None], (n_ctx, 1, 3 * D_MODEL))
        mod_s = mod[i, 1:1 + n_lat][:, None, :]
        g, b = ln_g[i][None], ln_b[i][None]
        if kind == 0:
            w_u = pool_w_in[j][:, :D_MODEL].astype(BF16)
            w_z = pool_w_in[j][:, D_MODEL:].astype(BF16)
            w_g = pool_w_grp[j].astype(BF16)
            w_o = pool_w_out[j].astype(BF16)
            ps = pool_scale[j][None]
            yp = _pool_layer(yp, mod_p, w_u, w_z, w_g, ps, w_o, g, b)
            ys = _pool_layer(ys, mod_s, w_u, w_z, w_g, ps, w_o, g, b)
        elif kind == 1:
            w1, w_uq = _mla_weights(mla_w_in[j], mla_w_uq[j])
            w_ukv = mla_w_ukv[j].astype(BF16)
            w_o = mla_w_out[j].astype(BF16)
            qn, kvn = mla_q_norm[j][None], mla_kv_norm[j][None]
            q_p, kv_p, krp_p, z_p, st_ckv, st_kr = _mla_proj(yp, mod_p, w1, qn, kvn, w_uq, w_ukv, None, True)
            yp = _mla_ctx_layer(q_p, z_p, kv_p, krp_p, yp, mod_p, w_o, g, b)
            q_s, kv_s, krp_s, z_s = _mla_proj(ys, mod_s, w1, qn, kvn, w_uq, w_ukv, _rope_table(lat_len), False)
            kv_c, krp_c = _mla_cache_keys(cache_mla_ckv, cache_mla_krope, j, w_ukv)
            a_s = _mla_attention(q_s, z_s, [(kv_c, krp_c), (kv_s, krp_s)], lat_len, 2, 2 * ATTN_ROWS)
            ys = _out_layer(a_s, ys, mod_s, w_o, g, b)
        else:
            w_in = na_w_in[j].astype(BF16)
            w_o = na_w_out[j].astype(BF16)
            q_p, k_p, v_p, z_p, st_k, st_v = _na_proj(yp, mod_p, w_in, True)
            yp = _na_ctx_layer(q_p, k_p, v_p, z_p, yp, mod_p, w_o, g, b)
            q_s, k_s, v_s, z_s = _na_proj(ys, mod_s, w_in, False)
            past = cache_na_k.shape[2]
            k_c = cache_na_k[:, j].reshape(n_lat, past, NA_WIDTH).astype(BF16)
            v_c = cache_na_v[:, j].reshape(n_lat, past, NA_WIDTH).astype(BF16)
            bias = _na_bias_table(na_rpb[j])
            a_s = _na_lat_attention(q_s, k_s, v_s, k_c, v_c, bias, z_s)
            ys = _out_layer(a_s, ys, mod_s, w_o, g, b)
    state_na_k = st_k.reshape(n_ctx, 1, -1, NA_HEADS, NA_HEAD_DIM)
    state_na_v = st_v.reshape(n_ctx, 1, -1, NA_HEADS, NA_HEAD_DIM)
    return (yp, ys, st_ckv, st_kr, state_na_k, state_na_v)
```

```python
import functools
import math

import numpy as np
import jax
import jax.numpy as jnp
from jax import lax
from jax.experimental import pallas as pl
from jax.experimental.pallas import tpu as pltpu

F32 = jnp.float32
BF16 = jnp.bfloat16

D_MODEL = 1024
DEPTH = 4
GRID_W = 64
N_MIXERS = 3

POOL_WINDOWS = (2, 4, 8, 16)
POOL_GROUP_DIM = D_MODEL // len(POOL_WINDOWS)
POOL_HALO = 8

MLA_HEADS = 8
QK_NOPE = 128
QK_ROPE = 64
V_HEAD = 128
Q_LORA = D_MODEL // 2
KV_LORA = D_MODEL // 4
MLA_WIDTH = MLA_HEADS * V_HEAD
MLA_SCALE = (QK_NOPE + QK_ROPE) ** -0.5

NA_HEADS = 16
NA_HEAD_DIM = 64
NA_WIDTH = NA_HEADS * NA_HEAD_DIM
NA_WIN_R = 8
NA_WIN_C = 16
NA_SCALE = NA_HEAD_DIM ** -0.5
NA_PAIRS = NA_HEADS // 2

ROPE_BASE = 10000.0
LN_EPS = 1e-5
RMS_EPS = 1e-6
NEG_INF = -1e30
DEEPNORM_ALPHA = (2 * DEPTH) ** 0.25
LOG2E = math.log2(math.e)

LANES = 128
ROW_TILE = 512
ATTN_ROWS = 256
VMEM_LIMIT = 56 * 1024 * 1024


def _params(n_axes):
    return pltpu.CompilerParams(dimension_semantics=("arbitrary",) * n_axes, vmem_limit_bytes=VMEM_LIMIT)


def _silu(x):
    h = 0.5 * x
    return h + h * jnp.tanh(h)


def _dot(a, b):
    return jnp.dot(a, b, preferred_element_type=F32)


def _dot_nt(a, b):
    return lax.dot_general(a, b, (((1,), (1,)), ((), ())), preferred_element_type=F32)


def _weight_spec(block_shape, index):
    return pl.BlockSpec(block_shape, lambda *_: index, pipeline_mode=pl.Buffered(1))


def _cast_weights_once(n_axes, pairs):
    first = functools.reduce(jnp.logical_and, [pl.program_id(a) == 0 for a in range(n_axes)])

    @pl.when(first)
    def _():
        for src, dst in pairs:
            dst[...] = src[...].reshape(dst.shape).astype(BF16)


def _modulation(mod_ref):
    m = mod_ref[0]
    return m[:, :D_MODEL], m[:, D_MODEL:2 * D_MODEL], m[:, 2 * D_MODEL:]


def _deepnorm_ln(x, gate, branch, g, b):
    xf = DEEPNORM_ALPHA * x + gate * branch
    mu = jnp.mean(xf, axis=-1, keepdims=True)
    xc = xf - mu
    var = jnp.mean(xc * xc, axis=-1, keepdims=True)
    return xc * lax.rsqrt(var + LN_EPS) * g + b


def _rms(x, g):
    return x * lax.rsqrt(jnp.mean(x * x, axis=-1, keepdims=True) + RMS_EPS) * g


def _ada_kernel(cond_ref, w_ref, b_ref, o_ref):
    c = cond_ref[...]
    o_ref[0] = _dot(_silu(c).astype(BF16), w_ref[0].astype(BF16)) + b_ref[0]


def _ada(cond, ada_w, ada_b):
    tn = D_MODEL
    return pl.pallas_call(
        _ada_kernel,
        grid=(DEPTH, 3 * D_MODEL // tn),
        in_specs=[
            pl.BlockSpec((8, D_MODEL), lambda l, n: (0, 0)),
            pl.BlockSpec((1, D_MODEL, tn), lambda l, n: (l, 0, n)),
            pl.BlockSpec((1, 1, tn), lambda l, n: (l, 0, n)),
        ],
        out_specs=pl.BlockSpec((1, 8, tn), lambda l, n: (l, 0, n)),
        out_shape=jax.ShapeDtypeStruct((DEPTH, 8, 3 * D_MODEL), F32),
        compiler_params=_params(2),
        name="ada",
    )(cond, ada_w, ada_b.reshape(DEPTH, 1, 3 * D_MODEL))


def _rows_ahead(x, k):
    return x if k == 0 else pltpu.roll(x, x.shape[0] - k, 0)


def _pool_kernel(seq_len, x_ref, xp_ref, xn_ref, mod_ref, wu32_ref, wz32_ref, wg32_ref, ps_ref, wo32_ref,
                 lng_ref, lnb_ref, o_ref, wu_ref, wz_ref, wg_ref, wo_ref):
    _cast_weights_once(2, [(wu32_ref, wu_ref), (wz32_ref, wz_ref), (wg32_ref, wg_ref), (wo32_ref, wo_ref)])
    t_rows = x_ref.shape[1]
    i = pl.program_id(1)
    n_tiles = pl.num_programs(1)
    shift, scale, gate = _modulation(mod_ref)
    one_scale = 1.0 + scale
    x = x_ref[0]
    hm = x * one_scale + shift
    hp = xp_ref[0] * one_scale + shift
    hn = xn_ref[0] * one_scale + shift
    hext = jnp.concatenate([hm, hp, hn], axis=0).astype(BF16)
    u = _dot(hext, wu_ref[...])
    z = _dot(hext[:t_rows], wz_ref[...])
    prev_ok = (i > 0).astype(F32)
    next_ok = (i < n_tiles - 1).astype(F32)
    centre = u[:t_rows]
    useq = jnp.concatenate([u[t_rows:t_rows + POOL_HALO] * prev_ok, centre, u[t_rows + POOL_HALO:] * next_ok], axis=0)

    t = i * t_rows + lax.broadcasted_iota(jnp.int32, (t_rows, 1), 0)
    mixed = []
    for g, w in enumerate(POOL_WINDOWS):
        cols = slice(g * POOL_GROUP_DIM, (g + 1) * POOL_GROUP_DIM)
        q = useq[:, cols]
        span = 1
        while span < w // 2:
            q = q + _rows_ahead(q, span)
            span *= 2
        acc = _rows_ahead(q, POOL_HALO - w // 2)[:t_rows] + q[POOL_HALO:POOL_HALO + t_rows]
        lo = jnp.clip(t - w // 2, 0, seq_len)
        hi = jnp.clip(t + (w - w // 2), 0, seq_len)
        inv_cnt = 1.0 / (hi - lo).astype(F32)
        mg = (acc * inv_cnt - centre[:, cols]).astype(BF16)
        mixed.append(_dot(mg, wg_ref[g]))
    mixed = jnp.concatenate(mixed, axis=1) * ps_ref[...]
    a = (mixed * _silu(z)).astype(BF16)
    branch = _dot(a, wo_ref[...])
    o_ref[0] = _deepnorm_ln(x, gate, branch, lng_ref[...], lnb_ref[...])


def _pool_layer(x, mod, slot, w_in, w_grp, p_scale, w_out, ln_g, ln_b):
    bsz, seq_len, _ = x.shape
    n_grp = len(POOL_WINDOWS)
    t_rows = min(ROW_TILE, seq_len)
    n_tiles = seq_len // t_rows
    halo_per_tile = t_rows // POOL_HALO
    halo_per_seq = seq_len // POOL_HALO
    n_halo = bsz * halo_per_seq
    xh = x.reshape(n_halo, POOL_HALO, D_MODEL)
    const2 = lambda b, i: (0, 0)
    return pl.pallas_call(
        functools.partial(_pool_kernel, seq_len),
        grid=(bsz, n_tiles),
        in_specs=[
            pl.BlockSpec((1, t_rows, D_MODEL), lambda b, i: (b, i, 0)),
            pl.BlockSpec((1, POOL_HALO, D_MODEL),
                         lambda b, i: (jnp.maximum(b * halo_per_seq + i * halo_per_tile - 1, 0), 0, 0)),
            pl.BlockSpec((1, POOL_HALO, D_MODEL),
                         lambda b, i: (jnp.minimum(b * halo_per_seq + (i + 1) * halo_per_tile, n_halo - 1), 0, 0)),
            pl.BlockSpec((1, 1, 3 * D_MODEL), lambda b, i: (b, 0, 0)),
            _weight_spec((1, D_MODEL, D_MODEL), (slot, 0, 0)),
            _weight_spec((1, D_MODEL, D_MODEL), (slot, 0, 1)),
            _weight_spec((1, n_grp, POOL_GROUP_DIM, POOL_GROUP_DIM), (slot, 0, 0, 0)),
            pl.BlockSpec((1, D_MODEL), const2),
            _weight_spec((1, D_MODEL, D_MODEL), (slot, 0, 0)),
            pl.BlockSpec((1, D_MODEL), const2),
            pl.BlockSpec((1, D_MODEL), const2),
        ],
        out_specs=pl.BlockSpec((1, t_rows, D_MODEL), lambda b, i: (b, i, 0)),
        out_shape=jax.ShapeDtypeStruct(x.shape, F32),
        scratch_shapes=[pltpu.VMEM((D_MODEL, D_MODEL), BF16),
                        pltpu.VMEM((D_MODEL, D_MODEL), BF16),
                        pltpu.VMEM((n_grp, POOL_GROUP_DIM, POOL_GROUP_DIM), BF16),
                        pltpu.VMEM((D_MODEL, D_MODEL), BF16)],
        compiler_params=_params(2),
        name="pool_layer",
    )(x, xh, xh, mod, w_in, w_in, w_grp, p_scale, w_out, ln_g, ln_b)


def _out_epilogue(a_ref, x_ref, mod_ref, wo_ref, lng_ref, lnb_ref):
    _, _, gate = _modulation(mod_ref)
    branch = _dot(a_ref[0], wo_ref[...])
    return _deepnorm_ln(x_ref[0], gate, branch, lng_ref[...], lnb_ref[...])


def _out_kernel(a_ref, x_ref, mod_ref, wo32_ref, lng_ref, lnb_ref, o_ref, wo_ref):
    _cast_weights_once(2, [(wo32_ref, wo_ref)])
    o_ref[0] = _out_epilogue(a_ref, x_ref, mod_ref, wo_ref, lng_ref, lnb_ref)


def _out_layer(a, x, mod, slot, w_out, ln_g, ln_b):
    bsz, seq_len, _ = x.shape
    t_rows = min(ROW_TILE, seq_len)
    const2 = lambda b, i: (0, 0)
    tok = pl.BlockSpec((1, t_rows, D_MODEL), lambda b, i: (b, i, 0))
    return pl.pallas_call(
        _out_kernel,
        grid=(bsz, seq_len // t_rows),
        in_specs=[tok, tok,
                  pl.BlockSpec((1, 1, 3 * D_MODEL), lambda b, i: (b, 0, 0)),
                  _weight_spec((1, D_MODEL, D_MODEL), (slot, 0, 0)),
                  pl.BlockSpec((1, D_MODEL), const2),
                  pl.BlockSpec((1, D_MODEL), const2)],
        out_specs=tok,
        out_shape=jax.ShapeDtypeStruct(x.shape, F32),
        scratch_shapes=[pltpu.VMEM((D_MODEL, D_MODEL), BF16)],
        compiler_params=_params(2),
        name="out_layer",
    )(a, x, mod, w_out, ln_g, ln_b)


MLA_W1_COLS = Q_LORA + KV_LORA + 2 * QK_ROPE + MLA_WIDTH
MLA_Z_OFF = Q_LORA + KV_LORA + 2 * QK_ROPE
MLA_Q_COLS = MLA_HEADS * QK_NOPE + MLA_HEADS * LANES
MLA_Q_SCALE = MLA_SCALE * LOG2E


def _rope_block(blk, cs, low_half):
    if cs is not None:
        blk = blk * cs
        blk = blk + pltpu.roll(blk, LANES // 2, 1)
    return jnp.where(low_half, blk, 0.0)


def _mla_proj_kernel(rope, emit_state, *refs):
    x_ref, mod_ref, w1_ref, qn_ref, kvn_ref, wuq_ref, wukv_ref = refs[:7]
    refs = refs[7:]
    cs = None
    if rope:
        cs = refs[0][...]
        refs = refs[1:]
    q_ref, kv_ref, krp_ref, z_ref = refs[:4]
    shift, scale, _ = _modulation(mod_ref)
    h = (x_ref[0] * (1.0 + scale) + shift).astype(BF16)
    p = _dot(h, w1_ref[...])
    cq = p[:, :Q_LORA]
    ckv = p[:, Q_LORA:Q_LORA + KV_LORA]
    krb = p[:, Q_LORA + KV_LORA:MLA_Z_OFF]
    z_ref[0] = p[:, MLA_Z_OFF:]
    ckvn = _rms(ckv, kvn_ref[...])
    if emit_state:
        ckv_ref, kr_ref = refs[4:6]
        ckv_ref[0, 0] = ckvn
        kr_ref[0, 0] = krb[:, :QK_ROPE]
    low_half = lax.broadcasted_iota(jnp.int32, (1, LANES), 1) < LANES // 2
    krp_ref[0] = _rope_block(krb, cs, low_half).astype(BF16)
    kv_ref[0] = _dot(ckvn.astype(BF16), wukv_ref[...]).astype(BF16)
    qf = _dot(_rms(cq, qn_ref[...]).astype(BF16), wuq_ref[...])
    n_nope = MLA_HEADS * QK_NOPE
    q_ref[0, :, :n_nope] = (qf[:, :n_nope] * MLA_Q_SCALE).astype(BF16)
    for hh in range(MLA_HEADS):
        cols = slice(n_nope + hh * LANES, n_nope + (hh + 1) * LANES)
        q_ref[0, :, cols] = (_rope_block(qf[:, cols], cs, low_half) * MLA_Q_SCALE).astype(BF16)


def _mla_proj(x, mod, w1, q_norm, kv_norm, w_uq, w_ukv, rope_cs, emit_state):
    bsz, seq_len, _ = x.shape
    t_rows = min(ROW_TILE, seq_len)
    const2 = lambda b, i: (0, 0)
    tok = lambda width: pl.BlockSpec((1, t_rows, width), lambda b, i: (b, i, 0))
    in_specs = [tok(D_MODEL),
                pl.BlockSpec((1, 1, 3 * D_MODEL), lambda b, i: (b, 0, 0)),
                pl.BlockSpec((D_MODEL, MLA_W1_COLS), const2),
                pl.BlockSpec((1, Q_LORA), const2),
                pl.BlockSpec((1, KV_LORA), const2),
                pl.BlockSpec((Q_LORA, MLA_Q_COLS), const2),
                pl.BlockSpec((KV_LORA, MLA_HEADS * (QK_NOPE + V_HEAD)), const2)]
    args = [x, mod, w1, q_norm, kv_norm, w_uq, w_ukv]
    if rope_cs is not None:
        in_specs.append(pl.BlockSpec((t_rows, LANES), lambda b, i: (i, 0)))
        args.append(rope_cs)
    out_specs = [tok(MLA_Q_COLS), tok(MLA_HEADS * (QK_NOPE + V_HEAD)), tok(LANES), tok(MLA_WIDTH)]
    out_shape = [jax.ShapeDtypeStruct((bsz, seq_len, MLA_Q_COLS), BF16),
                 jax.ShapeDtypeStruct((bsz, seq_len, MLA_HEADS * (QK_NOPE + V_HEAD)), BF16),
                 jax.ShapeDtypeStruct((bsz, seq_len, LANES), BF16),
                 jax.ShapeDtypeStruct((bsz, seq_len, MLA_WIDTH), F32)]
    if emit_state:
        out_specs += [pl.BlockSpec((1, 1, t_rows, KV_LORA), lambda b, i: (b, 0, i, 0)),
                      pl.BlockSpec((1, 1, t_rows, QK_ROPE), lambda b, i: (b, 0, i, 0))]
        out_shape += [jax.ShapeDtypeStruct((bsz, 1, seq_len, KV_LORA), F32),
                      jax.ShapeDtypeStruct((bsz, 1, seq_len, QK_ROPE), F32)]
    return pl.pallas_call(
        functools.partial(_mla_proj_kernel, rope_cs is not None, emit_state),
        grid=(bsz, seq_len // t_rows),
        in_specs=in_specs,
        out_specs=out_specs,
        out_shape=out_shape,
        compiler_params=_params(2),
        name="mla_proj",
    )(*args)


def _mla_cache_kernel(ckv_ref, kr_ref, wukv_ref, kv_ref, krp_ref):
    kv_ref[0] = _dot(ckv_ref[0, 0].astype(BF16), wukv_ref[...]).astype(BF16)
    krp_ref[0, :, :QK_ROPE] = kr_ref[0, 0].astype(BF16)
    krp_ref[0, :, QK_ROPE:] = jnp.zeros((kr_ref.shape[2], LANES - QK_ROPE), BF16)


def _mla_cache_keys(cache_ckv, cache_kr, slot, w_ukv):
    bsz, _, past, _ = cache_ckv.shape
    kv_cols = MLA_HEADS * (QK_NOPE + V_HEAD)
    return pl.pallas_call(
        _mla_cache_kernel,
        grid=(bsz,),
        in_specs=[pl.BlockSpec((1, 1, past, KV_LORA), lambda b: (b, slot, 0, 0)),
                  pl.BlockSpec((1, 1, past, QK_ROPE), lambda b: (b, slot, 0, 0)),
                  pl.BlockSpec((KV_LORA, kv_cols), lambda b: (0, 0))],
        out_specs=[pl.BlockSpec((1, past, kv_cols), lambda b: (b, 0, 0)),
                   pl.BlockSpec((1, past, LANES), lambda b: (b, 0, 0))],
        out_shape=[jax.ShapeDtypeStruct((bsz, past, kv_cols), BF16),
                   jax.ShapeDtypeStruct((bsz, past, LANES), BF16)],
        compiler_params=_params(1),
        name="mla_cache_keys",
    )(cache_ckv, cache_kr, w_ukv)


def _softmax_pv(scores, values):
    m = functools.reduce(jnp.maximum, [jnp.max(s, axis=1, keepdims=True) for s in scores])
    denom = 0.0
    acc = 0.0
    for s, v in zip(scores, values):
        p = jnp.exp2(s - m)
        denom = denom + jnp.sum(p, axis=1, keepdims=True)
        acc = acc + _dot(p.astype(BF16), v)
    return acc / denom


def _mla_attn_kernel(n_seg, heads, sub_rows, qn_ref, qr_ref, z_ref, *refs):
    o_ref = refs[2 * n_seg]
    q_rows = qn_ref.shape[1]
    kv_w = QK_NOPE + V_HEAD
    for h in range(heads):
        lanes = slice(h * LANES, (h + 1) * LANES)
        keys, values = [], []
        for s in range(n_seg):
            kv_ref, kr_ref = refs[2 * s:2 * s + 2]
            keys.append(jnp.concatenate([kv_ref[0, :, h * kv_w:h * kv_w + QK_NOPE], kr_ref[0]], axis=1))
            values.append(kv_ref[0, :, h * kv_w + QK_NOPE:(h + 1) * kv_w])
        for r0 in range(0, q_rows, sub_rows):
            rows = slice(r0, r0 + sub_rows)
            q = jnp.concatenate([qn_ref[0, rows, lanes], qr_ref[0, rows, lanes]], axis=1)
            o = _softmax_pv([_dot_nt(q, k) for k in keys], values)
            o_ref[0, rows, lanes] = (o * _silu(z_ref[0, rows, lanes])).astype(BF16)


def _mla_attention(q, z, segments, q_tile, heads, sub_rows):
    bsz, lq, _ = q.shape
    n_hblk = MLA_HEADS // heads
    hw = heads * LANES
    in_specs = [pl.BlockSpec((1, q_tile, hw), lambda b, h, i: (b, i, h)),
                pl.BlockSpec((1, q_tile, hw), lambda b, h, i: (b, i, n_hblk + h)),
                pl.BlockSpec((1, q_tile, hw), lambda b, h, i: (b, i, h))]
    args = [q, q, z]
    for kv, krp in segments:
        lk = kv.shape[1]
        in_specs += [pl.BlockSpec((1, lk, 2 * hw), lambda b, h, i: (b, 0, h)),
                     pl.BlockSpec((1, lk, LANES), lambda b, h, i: (b, 0, 0))]
        args += [kv, krp]
    return pl.pallas_call(
        functools.partial(_mla_attn_kernel, len(segments), heads, sub_rows),
        grid=(bsz, n_hblk, lq // q_tile),
        in_specs=in_specs,
        out_specs=pl.BlockSpec((1, q_tile, hw), lambda b, h, i: (b, i, h)),
        out_shape=jax.ShapeDtypeStruct((bsz, lq, MLA_WIDTH), BF16),
        compiler_params=_params(3),
        name="mla_attention",
    )(*args)


def _mla_ctx_kernel(qn_ref, qr_ref, z_ref, kv_ref, kr_ref, x_ref, mod_ref, wo32_ref, lng_ref, lnb_ref, y_ref,
                    a_ref, wo_ref):
    _cast_weights_once(1, [(wo32_ref, wo_ref)])
    _mla_attn_kernel(1, MLA_HEADS, ATTN_ROWS, qn_ref, qr_ref, z_ref, kv_ref, kr_ref, a_ref)
    y_ref[0] = _out_epilogue(a_ref, x_ref, mod_ref, wo_ref, lng_ref, lnb_ref)


def _ctx_fused_specs(seq_len, slot):
    const2 = lambda b: (0, 0)
    tok = pl.BlockSpec((1, seq_len, D_MODEL), lambda b: (b, 0, 0))
    return tok, [tok,
                 pl.BlockSpec((1, 1, 3 * D_MODEL), lambda b: (b, 0, 0)),
                 _weight_spec((1, D_MODEL, D_MODEL), (slot, 0, 0)),
                 pl.BlockSpec((1, D_MODEL), const2),
                 pl.BlockSpec((1, D_MODEL), const2)]


def _mla_ctx_layer(q, z, kv, krp, x, mod, slot, w_out, ln_g, ln_b):
    bsz, seq_len, _ = x.shape
    n_nope = MLA_HEADS * QK_NOPE
    tok, tail_specs = _ctx_fused_specs(seq_len, slot)
    return pl.pallas_call(
        _mla_ctx_kernel,
        grid=(bsz,),
        in_specs=[pl.BlockSpec((1, seq_len, n_nope), lambda b: (b, 0, 0)),
                  pl.BlockSpec((1, seq_len, n_nope), lambda b: (b, 0, 1)),
                  tok,
                  pl.BlockSpec((1, seq_len, kv.shape[2]), lambda b: (b, 0, 0)),
                  pl.BlockSpec((1, seq_len, LANES), lambda b: (b, 0, 0))] + tail_specs,
        out_specs=tok,
        out_shape=jax.ShapeDtypeStruct(x.shape, F32),
        scratch_shapes=[pltpu.VMEM((1, seq_len, MLA_WIDTH), BF16), pltpu.VMEM((D_MODEL, D_MODEL), BF16)],
        compiler_params=_params(1),
        name="mla_ctx_layer",
    )(q, q, z, kv, krp, x, mod, w_out, ln_g, ln_b)


NA_Q_SCALE = NA_SCALE * LOG2E


def _na_proj_kernel(emit_state, fuse_prev, *refs):
    if fuse_prev:
        a_ref, x_ref, pmod_ref, pwo32_ref, plng_ref, plnb_ref, mod_ref, w32_ref = refs[:8]
        refs = refs[8:]
    else:
        x_ref, mod_ref, w32_ref = refs[:3]
        refs = refs[3:]
    q_ref, k_ref, v_ref, z_ref = refs[:4]
    refs = refs[4:]
    if emit_state:
        sk_ref, sv_ref = refs[:2]
        refs = refs[2:]
    if fuse_prev:
        y_ref, w_ref, pwo_ref = refs
        _cast_weights_once(2, [(w32_ref, w_ref), (pwo32_ref, pwo_ref)])
        x = _out_epilogue(a_ref, x_ref, pmod_ref, pwo_ref, plng_ref, plnb_ref)
        y_ref[0] = x
    else:
        (w_ref,) = refs
        _cast_weights_once(2, [(w32_ref, w_ref)])
        x = x_ref[0]
    shift, scale, _ = _modulation(mod_ref)
    h = (x * (1.0 + scale) + shift).astype(BF16)
    w = NA_WIDTH
    q_ref[0] = (_dot(h, w_ref[:, :w]) * NA_Q_SCALE).astype(BF16)
    k = _dot(h, w_ref[:, w:2 * w])
    v = _dot(h, w_ref[:, 2 * w:3 * w])
    k_ref[0] = k.astype(BF16)
    v_ref[0] = v.astype(BF16)
    z_ref[0] = _dot(h, w_ref[:, 3 * w:])
    if emit_state:
        sk_ref[0, 0] = k
        sv_ref[0, 0] = v


def _na_proj(x, mod, slot, w_in, emit_state, prev=None):
    bsz, seq_len, _ = x.shape
    t_rows = min(ROW_TILE, seq_len)
    const2 = lambda b, i: (0, 0)
    tok = pl.BlockSpec((1, t_rows, NA_WIDTH), lambda b, i: (b, i, 0))
    mod_spec = pl.BlockSpec((1, 1, 3 * D_MODEL), lambda b, i: (b, 0, 0))
    w_spec = _weight_spec((1, D_MODEL, 4 * NA_WIDTH), (slot, 0, 0))
    out_specs = [tok, tok, tok, tok]
    act = jax.ShapeDtypeStruct((bsz, seq_len, NA_WIDTH), BF16)
    out_shape = [act, act, act, jax.ShapeDtypeStruct((bsz, seq_len, NA_WIDTH), F32)]
    scratch = [pltpu.VMEM((D_MODEL, 4 * NA_WIDTH), BF16)]
    if emit_state:
        st = pl.BlockSpec((1, 1, t_rows, NA_WIDTH), lambda b, i: (b, 0, i, 0))
        out_specs += [st, st]
        out_shape += [jax.ShapeDtypeStruct((bsz, 1, seq_len, NA_WIDTH), F32)] * 2
    if prev is None:
        in_specs = [tok, mod_spec, w_spec]
        args = [x, mod, w_in]
    else:
        a, mod_prev, slot_prev, w_out_prev, ln_g_prev, ln_b_prev = prev
        in_specs = [tok, tok, mod_spec, _weight_spec((1, D_MODEL, D_MODEL), (slot_prev, 0, 0)),
                    pl.BlockSpec((1, D_MODEL), const2), pl.BlockSpec((1, D_MODEL), const2), mod_spec, w_spec]
        args = [a, x, mod_prev, w_out_prev, ln_g_prev, ln_b_prev, mod, w_in]
        out_specs.append(tok)
        out_shape.append(jax.ShapeDtypeStruct(x.shape, F32))
        scratch.append(pltpu.VMEM((D_MODEL, D_MODEL), BF16))
    return pl.pallas_call(
        functools.partial(_na_proj_kernel, emit_state, prev is not None),
        grid=(bsz, seq_len // t_rows),
        in_specs=in_specs,
        out_specs=out_specs,
        out_shape=out_shape,
        scratch_shapes=scratch,
        compiler_params=_params(2),
        name="na_proj",
    )(*args)


def _split_head_pair(q2, low_half):
    zero = jnp.zeros_like(q2)
    return jnp.concatenate([jnp.where(low_half, q2, zero), jnp.where(low_half, zero, q2)], axis=0)


def _na_ctx_attn_kernel(q_ref, k_ref, v_ref, z_ref, o_ref):
    n = q_ref.shape[1]
    low_half = lax.broadcasted_iota(jnp.int32, (1, LANES), 1) < NA_HEAD_DIM
    for j in range(NA_PAIRS):
        lanes = slice(j * LANES, (j + 1) * LANES)
        qs = _split_head_pair(q_ref[0, :, lanes], low_half)
        o = _softmax_pv([_dot_nt(qs, k_ref[0, :, lanes])], [v_ref[0, :, lanes]])
        o2 = jnp.where(low_half, o[:n], o[n:])
        o_ref[0, :, lanes] = (o2 * _silu(z_ref[0, :, lanes])).astype(BF16)


def _na_ctx_kernel(q_ref, k_ref, v_ref, z_ref, x_ref, mod_ref, wo32_ref, lng_ref, lnb_ref, y_ref, a_ref, wo_ref):
    _cast_weights_once(1, [(wo32_ref, wo_ref)])
    _na_ctx_attn_kernel(q_ref, k_ref, v_ref, z_ref, a_ref)
    y_ref[0] = _out_epilogue(a_ref, x_ref, mod_ref, wo_ref, lng_ref, lnb_ref)


def _na_ctx_layer(q, k, v, z, x, mod, slot, w_out, ln_g, ln_b):
    bsz, seq_len, _ = x.shape
    tok, tail_specs = _ctx_fused_specs(seq_len, slot)
    return pl.pallas_call(
        _na_ctx_kernel,
        grid=(bsz,),
        in_specs=[tok, tok, tok, tok] + tail_specs,
        out_specs=tok,
        out_shape=jax.ShapeDtypeStruct(x.shape, F32),
        scratch_shapes=[pltpu.VMEM((1, seq_len, NA_WIDTH), BF16), pltpu.VMEM((D_MODEL, D_MODEL), BF16)],
        compiler_params=_params(1),
        name="na_ctx_layer",
    )(q, k, v, z, x, mod, w_out, ln_g, ln_b)


NA_N_DR = 2 * NA_WIN_R - 1
NA_BIAS_SLOTS = NA_N_DR + 1


def _na_bias_kernel(rpb_ref, o_ref):
    qc = lax.broadcasted_iota(jnp.int32, (GRID_W, LANES), 0)
    lane = lax.broadcasted_iota(jnp.int32, (GRID_W, LANES), 1)
    low_half = lane < GRID_W
    kc = jnp.where(low_half, lane, lane - GRID_W)
    cstart = jnp.clip(qc - NA_WIN_C // 2, 0, GRID_W - NA_WIN_C)
    valid = (kc >= cstart) & (kc < cstart + NA_WIN_C)
    for i in range(NA_BIAS_SLOTS):
        row_l = jnp.broadcast_to(rpb_ref[0, pl.ds(max(i - 1, 0), 1), :], (GRID_W, LANES))
        row_r = jnp.broadcast_to(rpb_ref[0, pl.ds(min(i, NA_N_DR - 1), 1), :], (GRID_W, LANES))
        left = pltpu.roll(row_l, LANES - (NA_WIN_C - 1), 1, stride=1, stride_axis=0)
        right = pltpu.roll(row_r, GRID_W - (NA_WIN_C - 1), 1, stride=1, stride_axis=0)
        tile = jnp.where(low_half, left, right)
        o_ref[0, i] = jnp.where(valid, tile * LOG2E, NEG_INF)


def _na_bias_table(rpb):
    n_dr, n_dc = rpb.shape[1:]
    rpb_rows = jnp.pad(rpb, ((0, 0), (0, 16 - n_dr), (0, LANES - n_dc)))
    return pl.pallas_call(
        _na_bias_kernel,
        grid=(NA_HEADS,),
        in_specs=[pl.BlockSpec((1, 16, LANES), lambda h: (h, 0, 0))],
        out_specs=pl.BlockSpec((1, NA_BIAS_SLOTS, GRID_W, LANES), lambda h: (h, 0, 0, 0)),
        out_shape=jax.ShapeDtypeStruct((NA_HEADS, NA_BIAS_SLOTS, GRID_W, LANES), F32),
        compiler_params=_params(1),
        name="na_bias_table",
    )(rpb_rows)


NA_ROWS_PER_GROUP = 4
NA_BAND_ROWS = NA_ROWS_PER_GROUP + NA_WIN_R
NA_GROUPS_PER_STEP = 8


def _na_lat_attn_kernel(n_rows, n_groups, q_ref, k_ref, v_ref, kc_ref, vc_ref, b_ref, z_ref, o_ref):
    n_q = NA_ROWS_PER_GROUP * GRID_W
    low_half = lax.broadcasted_iota(jnp.int32, (1, LANES), 1) < NA_HEAD_DIM
    for grp in range(n_groups):
        rows = slice(grp * n_q, (grp + 1) * n_q)
        r0 = (pl.program_id(2) * n_groups + grp) * NA_ROWS_PER_GROUP
        band0 = jnp.clip(r0 - NA_WIN_R // 2, 0, n_rows - NA_BAND_ROWS)
        qs = _split_head_pair(q_ref[0, rows, :], low_half)
        key0 = pl.multiple_of(band0 * GRID_W, GRID_W)
        kl = k_ref[0, pl.ds(key0, NA_BAND_ROWS * GRID_W), :]
        vl = v_ref[0, pl.ds(key0, NA_BAND_ROWS * GRID_W), :]
        bias_rows = []
        for rr in range(NA_ROWS_PER_GROUP):
            r = r0 + rr
            win0 = jnp.clip(r - NA_WIN_R // 2, 0, n_rows - NA_WIN_R)
            blocks = []
            for cblk in range(NA_BAND_ROWS // 2):
                key_row = band0 + 2 * cblk
                slot = jnp.clip(key_row - r + NA_WIN_R, 0, NA_BIAS_SLOTS - 1)
                left_pen = jnp.where((key_row >= win0) & (key_row < win0 + NA_WIN_R), 0.0, NEG_INF)
                right_pen = jnp.where((key_row + 1 >= win0) & (key_row + 1 < win0 + NA_WIN_R), 0.0, NEG_INF)
                blocks.append((slot, jnp.where(low_half, left_pen, right_pen)))
            bias_rows.append(blocks)
        bias = jnp.concatenate(
            [jnp.concatenate(
                [jnp.concatenate([b_ref[e, slot] + pen for slot, pen in blocks], axis=1)
                 for blocks in bias_rows], axis=0)
             for e in range(2)], axis=0)
        o = _softmax_pv([_dot_nt(qs, kl) + bias, _dot_nt(qs, kc_ref[0])], [vl, vc_ref[0]])
        o2 = jnp.where(low_half, o[:n_q], o[n_q:])
        o_ref[0, rows, :] = (o2 * _silu(z_ref[0, rows, :])).astype(BF16)


def _na_lat_attention(q, k, v, k_ctx, v_ctx, bias, z):
    bsz, seq_len, _ = q.shape
    n_rows = seq_len // GRID_W
    past = k_ctx.shape[1]
    n_groups = min(NA_GROUPS_PER_STEP, n_rows // NA_ROWS_PER_GROUP)
    rows_per_step = n_groups * NA_ROWS_PER_GROUP
    row_blk = pl.BlockSpec((1, rows_per_step * GRID_W, LANES), lambda b, j, r: (b, r, j))
    seq_blk = pl.BlockSpec((1, seq_len, LANES), lambda b, j, r: (b, 0, j))
    ctx_blk = pl.BlockSpec((1, past, LANES), lambda b, j, r: (b, 0, j))
    return pl.pallas_call(
        functools.partial(_na_lat_attn_kernel, n_rows, n_groups),
        grid=(bsz, NA_PAIRS, n_rows // rows_per_step),
        in_specs=[row_blk, seq_blk, seq_blk, ctx_blk, ctx_blk,
                  pl.BlockSpec((2, NA_BIAS_SLOTS, GRID_W, LANES), lambda b, j, r: (j, 0, 0, 0)),
                  row_blk],
        out_specs=row_blk,
        out_shape=jax.ShapeDtypeStruct((bsz, seq_len, NA_WIDTH), BF16),
        compiler_params=_params(3),
        name="na_lat_attention",
    )(q, k, v, k_ctx, v_ctx, bias, z)


def _rope_swap_perm():
    nf = QK_ROPE // 4
    idx = np.arange(QK_ROPE).reshape(2, 2, nf)
    return idx[:, ::-1, :].reshape(-1)


def _rope_table(seq_len):
    nf = QK_ROPE // 4
    inv = ROPE_BASE ** (-jnp.arange(nf, dtype=F32) / nf)
    t = jnp.arange(seq_len)
    pos = jnp.stack([t // GRID_W, t % GRID_W], -1).astype(F32)
    ang = pos[:, :, None] * inv
    cos = jnp.cos(ang)
    sin = jnp.sin(ang)
    c = jnp.concatenate([cos, cos], axis=-1).reshape(seq_len, QK_ROPE)
    s = jnp.concatenate([-sin, sin], axis=-1).reshape(seq_len, QK_ROPE)
    return jnp.concatenate([c, s], axis=-1)


def _mla_weights(w_in, w_uq):
    perm = _rope_swap_perm()
    o1, o2, o3 = Q_LORA, Q_LORA + KV_LORA, Q_LORA + KV_LORA + QK_ROPE
    w_kr = w_in[:, o2:o3]
    w1 = jnp.concatenate([w_in[:, :o2], w_kr, w_kr[:, perm], w_in[:, o3:]], axis=1).astype(BF16)
    wq = w_uq.reshape(Q_LORA, MLA_HEADS, QK_NOPE + QK_ROPE)
    nope = wq[:, :, :QK_NOPE].reshape(Q_LORA, MLA_HEADS * QK_NOPE)
    rope = wq[:, :, QK_NOPE:]
    rope_blk = jnp.concatenate([rope, rope[:, :, perm]], axis=-1).reshape(Q_LORA, MLA_HEADS * LANES)
    return w1, jnp.concatenate([nope, rope_blk], axis=1).astype(BF16)


def kernel(x_prompt, x_sample, cache_mla_ckv, cache_mla_krope, cache_na_k, cache_na_v, c, c_ctx, ada_w, ada_b, ln_g, ln_b, pool_w_in, pool_w_grp, pool_scale, pool_w_out, mla_w_in, mla_q_norm, mla_w_uq, mla_kv_norm, mla_w_ukv, mla_w_out, na_w_in, na_rpb, na_w_out):
    n_ctx, n_lat = x_prompt.shape[0], x_sample.shape[0]
    lat_len = x_sample.shape[1]
    cond = jnp.concatenate([c_ctx[None], c, jnp.zeros((8 - 1 - n_lat, D_MODEL), F32)], axis=0)
    mod = _ada(cond, ada_w, ada_b)
    yp, ys = x_prompt, x_sample
    st_ckv = st_kr = st_k = st_v = None
    ctx_len = x_prompt.shape[1]
    ctx_fold = 2 if (n_ctx % 2 == 0 and 2 * ctx_len <= ROW_TILE) else 1
    fold = lambda t: t.reshape((n_ctx // ctx_fold, ctx_fold * ctx_len) + t.shape[2:])
    unfold = lambda t: t.reshape((n_ctx, ctx_len) + t.shape[2:])
    unfold_state = lambda t: t.reshape((n_ctx, 1, ctx_len) + t.shape[3:])
    pending = None
    for i in range(DEPTH):
        kind, j = i % N_MIXERS, i // N_MIXERS
        mod_p = jnp.broadcast_to(mod[i, 0:1][None], (n_ctx, 1, 3 * D_MODEL))
        mod_pf = mod_p[:n_ctx // ctx_fold]
        mod_s = mod[i, 1:1 + n_lat][:, None, :]
        g, b = ln_g[i][None], ln_b[i][None]
        if kind != 2 and pending is not None:
            ys = _out_layer(pending[0], ys, *pending[1:])
            pending = None
        if kind == 0:
            ps = pool_scale[j][None]
            yp = _pool_layer(yp, mod_p, j, pool_w_in, pool_w_grp, ps, pool_w_out, g, b)
            ys = _pool_layer(ys, mod_s, j, pool_w_in, pool_w_grp, ps, pool_w_out, g, b)
        elif kind == 1:
            w1, w_uq = _mla_weights(mla_w_in[j], mla_w_uq[j])
            w_ukv = mla_w_ukv[j].astype(BF16)
            qn, kvn = mla_q_norm[j][None], mla_kv_norm[j][None]
            q_p, kv_p, krp_p, z_p, st_ckv, st_kr = _mla_proj(fold(yp), mod_pf, w1, qn, kvn, w_uq, w_ukv, None, True)
            st_ckv, st_kr = unfold_state(st_ckv), unfold_state(st_kr)
            yp = _mla_ctx_layer(unfold(q_p), unfold(z_p), unfold(kv_p), unfold(krp_p), yp, mod_p, j, mla_w_out, g, b)
            q_s, kv_s, krp_s, z_s = _mla_proj(ys, mod_s, w1, qn, kvn, w_uq, w_ukv, _rope_table(lat_len), False)
            kv_c, krp_c = _mla_cache_keys(cache_mla_ckv, cache_mla_krope, j, w_ukv)
            a_s = _mla_attention(q_s, z_s, [(kv_c, krp_c), (kv_s, krp_s)], lat_len, 2, 2 * ATTN_ROWS)
            pending = (a_s, mod_s, j, mla_w_out, g, b)
        else:
            q_p, k_p, v_p, z_p, st_k, st_v = _na_proj(fold(yp), mod_pf, j, na_w_in, True)
            st_k, st_v = unfold_state(st_k), unfold_state(st_v)
            yp = _na_ctx_layer(unfold(q_p), unfold(k_p), unfold(v_p), unfold(z_p), yp, mod_p, j, na_w_out, g, b)
            if pending is None:
                q_s, k_s, v_s, z_s = _na_proj(ys, mod_s, j, na_w_in, False)
            else:
                q_s, k_s, v_s, z_s, ys = _na_proj(ys, mod_s, j, na_w_in, False, prev=pending)
                pending = None
            past = cache_na_k.shape[2]
            k_c = cache_na_k[:, j].reshape(n_lat, past, NA_WIDTH).astype(BF16)
            v_c = cache_na_v[:, j].reshape(n_lat, past, NA_WIDTH).astype(BF16)
            bias = _na_bias_table(na_rpb[j])
            a_s = _na_lat_attention(q_s, k_s, v_s, k_c, v_c, bias, z_s)
            pending = (a_s, mod_s, j, na_w_out, g, b)
    if pending is not None:
        ys = _out_layer(pending[0], ys, *pending[1:])
    state_na_k = st_k.reshape(n_ctx, 1, -1, NA_HEADS, NA_HEAD_DIM)
    state_na_v = st_v.reshape(n_ctx, 1, -1, NA_HEADS, NA_HEAD_DIM)
    return (yp, ys, st_ckv, st_kr, state_na_k, state_na_v)
```

```python
import functools
import math

import numpy as np
import jax
import jax.numpy as jnp
from jax import lax
from jax.experimental import pallas as pl
from jax.experimental.pallas import tpu as pltpu

F32 = jnp.float32
BF16 = jnp.bfloat16

D_MODEL = 1024
DEPTH = 4
GRID_W = 64
N_MIXERS = 3

POOL_WINDOWS = (2, 4, 8, 16)
POOL_GROUP_DIM = D_MODEL // len(POOL_WINDOWS)
POOL_HALO = 8

MLA_HEADS = 8
QK_NOPE = 128
QK_ROPE = 64
V_HEAD = 128
Q_LORA = D_MODEL // 2
KV_LORA = D_MODEL // 4
MLA_WIDTH = MLA_HEADS * V_HEAD
MLA_SCALE = (QK_NOPE + QK_ROPE) ** -0.5

NA_HEADS = 16
NA_HEAD_DIM = 64
NA_WIDTH = NA_HEADS * NA_HEAD_DIM
NA_WIN_R = 8
NA_WIN_C = 16
NA_SCALE = NA_HEAD_DIM ** -0.5
NA_PAIRS = NA_HEADS // 2

ROPE_BASE = 10000.0
LN_EPS = 1e-5
RMS_EPS = 1e-6
NEG_INF = -1e30
DEEPNORM_ALPHA = (2 * DEPTH) ** 0.25
LOG2E = math.log2(math.e)

LANES = 128
ROW_TILE = 512
ATTN_ROWS = 256
VMEM_LIMIT = 56 * 1024 * 1024


def _params(n_axes):
    return pltpu.CompilerParams(dimension_semantics=("arbitrary",) * n_axes, vmem_limit_bytes=VMEM_LIMIT)


def _silu(x):
    h = 0.5 * x
    return h + h * jnp.tanh(h)


def _dot(a, b):
    return jnp.dot(a, b, preferred_element_type=F32)


def _dot_nt(a, b):
    return lax.dot_general(a, b, (((1,), (1,)), ((), ())), preferred_element_type=F32)


def _weight_spec(block_shape, index):
    return pl.BlockSpec(block_shape, lambda *_: index, pipeline_mode=pl.Buffered(1))


def _cast_weights_once(n_axes, pairs):
    first = functools.reduce(jnp.logical_and, [pl.program_id(a) == 0 for a in range(n_axes)])

    @pl.when(first)
    def _():
        for src, dst in pairs:
            dst[...] = src[...].reshape(dst.shape).astype(BF16)


def _state_out(bsz, seq_len, t_rows, n_seq):
    assert t_rows == seq_len and seq_len % n_seq == 0
    spec = lambda width: pl.BlockSpec((n_seq, 1, seq_len // n_seq, width), lambda b, i: (b, 0, 0, 0))
    shape = lambda width: jax.ShapeDtypeStruct((bsz * n_seq, 1, seq_len // n_seq, width), F32)
    return spec, shape


def _modulation(mod_ref):
    m = mod_ref[0]
    return m[:, :D_MODEL], m[:, D_MODEL:2 * D_MODEL], m[:, 2 * D_MODEL:]


def _deepnorm_ln(x, gate, branch, g, b):
    xf = DEEPNORM_ALPHA * x + gate * branch
    mu = jnp.mean(xf, axis=-1, keepdims=True)
    xc = xf - mu
    var = jnp.mean(xc * xc, axis=-1, keepdims=True)
    return xc * lax.rsqrt(var + LN_EPS) * g + b


def _rms(x, g):
    return x * lax.rsqrt(jnp.mean(x * x, axis=-1, keepdims=True) + RMS_EPS) * g


def _ada_kernel(cond_ref, w_ref, b_ref, o_ref):
    c = cond_ref[...]
    o_ref[0] = _dot(_silu(c).astype(BF16), w_ref[0].astype(BF16)) + b_ref[0]


def _ada(cond, ada_w, ada_b):
    tn = D_MODEL
    return pl.pallas_call(
        _ada_kernel,
        grid=(DEPTH, 3 * D_MODEL // tn),
        in_specs=[
            pl.BlockSpec((8, D_MODEL), lambda l, n: (0, 0)),
            pl.BlockSpec((1, D_MODEL, tn), lambda l, n: (l, 0, n)),
            pl.BlockSpec((1, 1, tn), lambda l, n: (l, 0, n)),
        ],
        out_specs=pl.BlockSpec((1, 8, tn), lambda l, n: (l, 0, n)),
        out_shape=jax.ShapeDtypeStruct((DEPTH, 8, 3 * D_MODEL), F32),
        compiler_params=_params(2),
        name="ada",
    )(cond, ada_w, ada_b.reshape(DEPTH, 1, 3 * D_MODEL))


def _rows_ahead(x, k):
    return x if k == 0 else pltpu.roll(x, x.shape[0] - k, 0)


def _pool_kernel(seq_len, x_ref, xp_ref, xn_ref, mod_ref, wu32_ref, wz32_ref, wg32_ref, ps_ref, wo32_ref,
                 lng_ref, lnb_ref, o_ref, wu_ref, wz_ref, wg_ref, wo_ref):
    _cast_weights_once(2, [(wu32_ref, wu_ref), (wz32_ref, wz_ref), (wg32_ref, wg_ref), (wo32_ref, wo_ref)])
    t_rows = x_ref.shape[1]
    i = pl.program_id(1)
    n_tiles = pl.num_programs(1)
    shift, scale, gate = _modulation(mod_ref)
    one_scale = 1.0 + scale
    x = x_ref[0]
    hm = x * one_scale + shift
    hp = xp_ref[0] * one_scale + shift
    hn = xn_ref[0] * one_scale + shift
    hext = jnp.concatenate([hm, hp, hn], axis=0).astype(BF16)
    u = _dot(hext, wu_ref[...])
    z = _dot(hext[:t_rows], wz_ref[...])
    prev_ok = (i > 0).astype(F32)
    next_ok = (i < n_tiles - 1).astype(F32)
    centre = u[:t_rows]
    useq = jnp.concatenate([u[t_rows:t_rows + POOL_HALO] * prev_ok, centre, u[t_rows + POOL_HALO:] * next_ok], axis=0)

    t = i * t_rows + lax.broadcasted_iota(jnp.int32, (t_rows, 1), 0)
    mixed = []
    for g, w in enumerate(POOL_WINDOWS):
        cols = slice(g * POOL_GROUP_DIM, (g + 1) * POOL_GROUP_DIM)
        q = useq[:, cols]
        span = 1
        while span < w // 2:
            q = q + _rows_ahead(q, span)
            span *= 2
        acc = _rows_ahead(q, POOL_HALO - w // 2)[:t_rows] + q[POOL_HALO:POOL_HALO + t_rows]
        lo = jnp.clip(t - w // 2, 0, seq_len)
        hi = jnp.clip(t + (w - w // 2), 0, seq_len)
        inv_cnt = 1.0 / (hi - lo).astype(F32)
        mg = (acc * inv_cnt - centre[:, cols]).astype(BF16)
        mixed.append(_dot(mg, wg_ref[g]))
    mixed = jnp.concatenate(mixed, axis=1) * ps_ref[...]
    a = (mixed * _silu(z)).astype(BF16)
    branch = _dot(a, wo_ref[...])
    o_ref[0] = _deepnorm_ln(x, gate, branch, lng_ref[...], lnb_ref[...])


def _pool_layer(x, mod, slot, w_in, w_grp, p_scale, w_out, ln_g, ln_b):
    bsz, seq_len, _ = x.shape
    n_grp = len(POOL_WINDOWS)
    t_rows = min(ROW_TILE, seq_len)
    n_tiles = seq_len // t_rows
    halo_per_tile = t_rows // POOL_HALO
    halo_per_seq = seq_len // POOL_HALO
    n_halo = bsz * halo_per_seq
    xh = x.reshape(n_halo, POOL_HALO, D_MODEL)
    const2 = lambda b, i: (0, 0)
    return pl.pallas_call(
        functools.partial(_pool_kernel, seq_len),
        grid=(bsz, n_tiles),
        in_specs=[
            pl.BlockSpec((1, t_rows, D_MODEL), lambda b, i: (b, i, 0)),
            pl.BlockSpec((1, POOL_HALO, D_MODEL),
                         lambda b, i: (jnp.maximum(b * halo_per_seq + i * halo_per_tile - 1, 0), 0, 0)),
            pl.BlockSpec((1, POOL_HALO, D_MODEL),
                         lambda b, i: (jnp.minimum(b * halo_per_seq + (i + 1) * halo_per_tile, n_halo - 1), 0, 0)),
            pl.BlockSpec((1, 1, 3 * D_MODEL), lambda b, i: (b, 0, 0)),
            _weight_spec((1, D_MODEL, D_MODEL), (slot, 0, 0)),
            _weight_spec((1, D_MODEL, D_MODEL), (slot, 0, 1)),
            _weight_spec((1, n_grp, POOL_GROUP_DIM, POOL_GROUP_DIM), (slot, 0, 0, 0)),
            pl.BlockSpec((1, D_MODEL), const2),
            _weight_spec((1, D_MODEL, D_MODEL), (slot, 0, 0)),
            pl.BlockSpec((1, D_MODEL), const2),
            pl.BlockSpec((1, D_MODEL), const2),
        ],
        out_specs=pl.BlockSpec((1, t_rows, D_MODEL), lambda b, i: (b, i, 0)),
        out_shape=jax.ShapeDtypeStruct(x.shape, F32),
        scratch_shapes=[pltpu.VMEM((D_MODEL, D_MODEL), BF16),
                        pltpu.VMEM((D_MODEL, D_MODEL), BF16),
                        pltpu.VMEM((n_grp, POOL_GROUP_DIM, POOL_GROUP_DIM), BF16),
                        pltpu.VMEM((D_MODEL, D_MODEL), BF16)],
        compiler_params=_params(2),
        name="pool_layer",
    )(x, xh, xh, mod, w_in, w_in, w_grp, p_scale, w_out, ln_g, ln_b)


def _out_epilogue(a_ref, x_ref, mod_ref, wo_ref, lng_ref, lnb_ref):
    _, _, gate = _modulation(mod_ref)
    branch = _dot(a_ref[0], wo_ref[...])
    return _deepnorm_ln(x_ref[0], gate, branch, lng_ref[...], lnb_ref[...])


def _out_kernel(a_ref, x_ref, mod_ref, wo32_ref, lng_ref, lnb_ref, o_ref, wo_ref):
    _cast_weights_once(2, [(wo32_ref, wo_ref)])
    o_ref[0] = _out_epilogue(a_ref, x_ref, mod_ref, wo_ref, lng_ref, lnb_ref)


def _out_layer(a, x, mod, slot, w_out, ln_g, ln_b):
    bsz, seq_len, _ = x.shape
    t_rows = min(ROW_TILE, seq_len)
    const2 = lambda b, i: (0, 0)
    tok = pl.BlockSpec((1, t_rows, D_MODEL), lambda b, i: (b, i, 0))
    return pl.pallas_call(
        _out_kernel,
        grid=(bsz, seq_len // t_rows),
        in_specs=[tok, tok,
                  pl.BlockSpec((1, 1, 3 * D_MODEL), lambda b, i: (b, 0, 0)),
                  _weight_spec((1, D_MODEL, D_MODEL), (slot, 0, 0)),
                  pl.BlockSpec((1, D_MODEL), const2),
                  pl.BlockSpec((1, D_MODEL), const2)],
        out_specs=tok,
        out_shape=jax.ShapeDtypeStruct(x.shape, F32),
        scratch_shapes=[pltpu.VMEM((D_MODEL, D_MODEL), BF16)],
        compiler_params=_params(2),
        name="out_layer",
    )(a, x, mod, w_out, ln_g, ln_b)


MLA_W1_COLS = Q_LORA + KV_LORA + 2 * QK_ROPE + MLA_WIDTH
MLA_Z_OFF = Q_LORA + KV_LORA + 2 * QK_ROPE
MLA_Q_COLS = MLA_HEADS * QK_NOPE + MLA_HEADS * LANES
MLA_Q_SCALE = MLA_SCALE * LOG2E


def _rope_block(blk, cs, low_half):
    if cs is not None:
        blk = blk * cs
        blk = blk + pltpu.roll(blk, LANES // 2, 1)
    return jnp.where(low_half, blk, 0.0)


def _mla_proj_kernel(rope, emit_state, *refs):
    x_ref, mod_ref, w1_ref, qn_ref, kvn_ref, wuq_ref, wukv_ref = refs[:7]
    refs = refs[7:]
    cs = None
    if rope:
        cs = refs[0][...]
        refs = refs[1:]
    q_ref, kv_ref, krp_ref, z_ref = refs[:4]
    shift, scale, _ = _modulation(mod_ref)
    h = (x_ref[0] * (1.0 + scale) + shift).astype(BF16)
    p = _dot(h, w1_ref[...])
    cq = p[:, :Q_LORA]
    ckv = p[:, Q_LORA:Q_LORA + KV_LORA]
    krb = p[:, Q_LORA + KV_LORA:MLA_Z_OFF]
    z_ref[0] = p[:, MLA_Z_OFF:]
    ckvn = _rms(ckv, kvn_ref[...])
    if emit_state:
        ckv_ref, kr_ref = refs[4:6]
        ckv_ref[...] = ckvn.reshape(ckv_ref.shape)
        kr_ref[...] = krb[:, :QK_ROPE].reshape(kr_ref.shape)
    low_half = lax.broadcasted_iota(jnp.int32, (1, LANES), 1) < LANES // 2
    krp_ref[0] = _rope_block(krb, cs, low_half).astype(BF16)
    kv_ref[0] = _dot(ckvn.astype(BF16), wukv_ref[...]).astype(BF16)
    qf = _dot(_rms(cq, qn_ref[...]).astype(BF16), wuq_ref[...])
    n_nope = MLA_HEADS * QK_NOPE
    q_ref[0, :, :n_nope] = (qf[:, :n_nope] * MLA_Q_SCALE).astype(BF16)
    for hh in range(MLA_HEADS):
        cols = slice(n_nope + hh * LANES, n_nope + (hh + 1) * LANES)
        q_ref[0, :, cols] = (_rope_block(qf[:, cols], cs, low_half) * MLA_Q_SCALE).astype(BF16)


def _mla_proj(x, mod, w1, q_norm, kv_norm, w_uq, w_ukv, rope_cs, emit_state):
    bsz, seq_len, _ = x.shape
    t_rows = min(ROW_TILE, seq_len)
    const2 = lambda b, i: (0, 0)
    tok = lambda width: pl.BlockSpec((1, t_rows, width), lambda b, i: (b, i, 0))
    in_specs = [tok(D_MODEL),
                pl.BlockSpec((1, 1, 3 * D_MODEL), lambda b, i: (b, 0, 0)),
                pl.BlockSpec((D_MODEL, MLA_W1_COLS), const2),
                pl.BlockSpec((1, Q_LORA), const2),
                pl.BlockSpec((1, KV_LORA), const2),
                pl.BlockSpec((Q_LORA, MLA_Q_COLS), const2),
                pl.BlockSpec((KV_LORA, MLA_HEADS * (QK_NOPE + V_HEAD)), const2)]
    args = [x, mod, w1, q_norm, kv_norm, w_uq, w_ukv]
    if rope_cs is not None:
        in_specs.append(pl.BlockSpec((t_rows, LANES), lambda b, i: (i, 0)))
        args.append(rope_cs)
    out_specs = [tok(MLA_Q_COLS), tok(MLA_HEADS * (QK_NOPE + V_HEAD)), tok(LANES), tok(MLA_WIDTH)]
    out_shape = [jax.ShapeDtypeStruct((bsz, seq_len, MLA_Q_COLS), BF16),
                 jax.ShapeDtypeStruct((bsz, seq_len, MLA_HEADS * (QK_NOPE + V_HEAD)), BF16),
                 jax.ShapeDtypeStruct((bsz, seq_len, LANES), BF16),
                 jax.ShapeDtypeStruct((bsz, seq_len, MLA_WIDTH), F32)]
    if emit_state:
        st_spec, st_shape = _state_out(bsz, seq_len, t_rows, emit_state)
        out_specs += [st_spec(KV_LORA), st_spec(QK_ROPE)]
        out_shape += [st_shape(KV_LORA), st_shape(QK_ROPE)]
    return pl.pallas_call(
        functools.partial(_mla_proj_kernel, rope_cs is not None, emit_state),
        grid=(bsz, seq_len // t_rows),
        in_specs=in_specs,
        out_specs=out_specs,
        out_shape=out_shape,
        compiler_params=_params(2),
        name="mla_proj",
    )(*args)


def _mla_cache_kernel(ckv_ref, kr_ref, wukv_ref, kv_ref, krp_ref):
    kv_ref[0] = _dot(ckv_ref[0, 0].astype(BF16), wukv_ref[...]).astype(BF16)
    krp_ref[0, :, :QK_ROPE] = kr_ref[0, 0].astype(BF16)
    krp_ref[0, :, QK_ROPE:] = jnp.zeros((kr_ref.shape[2], LANES - QK_ROPE), BF16)


def _mla_cache_keys(cache_ckv, cache_kr, slot, w_ukv):
    bsz, _, past, _ = cache_ckv.shape
    kv_cols = MLA_HEADS * (QK_NOPE + V_HEAD)
    return pl.pallas_call(
        _mla_cache_kernel,
        grid=(bsz,),
        in_specs=[pl.BlockSpec((1, 1, past, KV_LORA), lambda b: (b, slot, 0, 0)),
                  pl.BlockSpec((1, 1, past, QK_ROPE), lambda b: (b, slot, 0, 0)),
                  pl.BlockSpec((KV_LORA, kv_cols), lambda b: (0, 0))],
        out_specs=[pl.BlockSpec((1, past, kv_cols), lambda b: (b, 0, 0)),
                   pl.BlockSpec((1, past, LANES), lambda b: (b, 0, 0))],
        out_shape=[jax.ShapeDtypeStruct((bsz, past, kv_cols), BF16),
                   jax.ShapeDtypeStruct((bsz, past, LANES), BF16)],
        compiler_params=_params(1),
        name="mla_cache_keys",
    )(cache_ckv, cache_kr, w_ukv)


def _softmax_pv(scores, values):
    m = functools.reduce(jnp.maximum, [jnp.max(s, axis=1, keepdims=True) for s in scores])
    denom = 0.0
    acc = 0.0
    for s, v in zip(scores, values):
        p = jnp.exp2(s - m)
        denom = denom + jnp.sum(p, axis=1, keepdims=True)
        acc = acc + _dot(p.astype(BF16), v)
    return acc / denom


def _mla_attn_kernel(n_seg, heads, sub_rows, qn_ref, qr_ref, z_ref, *refs):
    o_ref = refs[2 * n_seg]
    q_rows = qn_ref.shape[1]
    kv_w = QK_NOPE + V_HEAD
    for h in range(heads):
        lanes = slice(h * LANES, (h + 1) * LANES)
        keys, values = [], []
        for s in range(n_seg):
            kv_ref, kr_ref = refs[2 * s:2 * s + 2]
            keys.append(jnp.concatenate([kv_ref[0, :, h * kv_w:h * kv_w + QK_NOPE], kr_ref[0]], axis=1))
            values.append(kv_ref[0, :, h * kv_w + QK_NOPE:(h + 1) * kv_w])
        for r0 in range(0, q_rows, sub_rows):
            rows = slice(r0, r0 + sub_rows)
            q = jnp.concatenate([qn_ref[0, rows, lanes], qr_ref[0, rows, lanes]], axis=1)
            o = _softmax_pv([_dot_nt(q, k) for k in keys], values)
            o_ref[0, rows, lanes] = (o * _silu(z_ref[0, rows, lanes])).astype(BF16)


def _mla_attention(q, z, segments, q_tile, heads, sub_rows):
    bsz, lq, _ = q.shape
    n_hblk = MLA_HEADS // heads
    hw = heads * LANES
    in_specs = [pl.BlockSpec((1, q_tile, hw), lambda b, h, i: (b, i, h)),
                pl.BlockSpec((1, q_tile, hw), lambda b, h, i: (b, i, n_hblk + h)),
                pl.BlockSpec((1, q_tile, hw), lambda b, h, i: (b, i, h))]
    args = [q, q, z]
    for kv, krp in segments:
        lk = kv.shape[1]
        in_specs += [pl.BlockSpec((1, lk, 2 * hw), lambda b, h, i: (b, 0, h)),
                     pl.BlockSpec((1, lk, LANES), lambda b, h, i: (b, 0, 0))]
        args += [kv, krp]
    return pl.pallas_call(
        functools.partial(_mla_attn_kernel, len(segments), heads, sub_rows),
        grid=(bsz, n_hblk, lq // q_tile),
        in_specs=in_specs,
        out_specs=pl.BlockSpec((1, q_tile, hw), lambda b, h, i: (b, i, h)),
        out_shape=jax.ShapeDtypeStruct((bsz, lq, MLA_WIDTH), BF16),
        compiler_params=_params(3),
        name="mla_attention",
    )(*args)


def _mla_ctx_kernel(qn_ref, qr_ref, z_ref, kv_ref, kr_ref, x_ref, mod_ref, wo32_ref, lng_ref, lnb_ref, y_ref,
                    a_ref, wo_ref):
    _cast_weights_once(1, [(wo32_ref, wo_ref)])
    _mla_attn_kernel(1, MLA_HEADS, ATTN_ROWS, qn_ref, qr_ref, z_ref, kv_ref, kr_ref, a_ref)
    y_ref[0] = _out_epilogue(a_ref, x_ref, mod_ref, wo_ref, lng_ref, lnb_ref)


def _ctx_fused_specs(seq_len, slot):
    const2 = lambda b: (0, 0)
    tok = pl.BlockSpec((1, seq_len, D_MODEL), lambda b: (b, 0, 0))
    return tok, [tok,
                 pl.BlockSpec((1, 1, 3 * D_MODEL), lambda b: (b, 0, 0)),
                 _weight_spec((1, D_MODEL, D_MODEL), (slot, 0, 0)),
                 pl.BlockSpec((1, D_MODEL), const2),
                 pl.BlockSpec((1, D_MODEL), const2)]


def _mla_ctx_layer(q, z, kv, krp, x, mod, slot, w_out, ln_g, ln_b):
    bsz, seq_len, _ = x.shape
    n_nope = MLA_HEADS * QK_NOPE
    tok, tail_specs = _ctx_fused_specs(seq_len, slot)
    return pl.pallas_call(
        _mla_ctx_kernel,
        grid=(bsz,),
        in_specs=[pl.BlockSpec((1, seq_len, n_nope), lambda b: (b, 0, 0)),
                  pl.BlockSpec((1, seq_len, n_nope), lambda b: (b, 0, 1)),
                  tok,
                  pl.BlockSpec((1, seq_len, kv.shape[2]), lambda b: (b, 0, 0)),
                  pl.BlockSpec((1, seq_len, LANES), lambda b: (b, 0, 0))] + tail_specs,
        out_specs=tok,
        out_shape=jax.ShapeDtypeStruct(x.shape, F32),
        scratch_shapes=[pltpu.VMEM((1, seq_len, MLA_WIDTH), BF16), pltpu.VMEM((D_MODEL, D_MODEL), BF16)],
        compiler_params=_params(1),
        name="mla_ctx_layer",
    )(q, q, z, kv, krp, x, mod, w_out, ln_g, ln_b)


NA_Q_SCALE = NA_SCALE * LOG2E


def _na_proj_kernel(emit_state, fuse_prev, *refs):
    if fuse_prev:
        a_ref, x_ref, pmod_ref, pwo32_ref, plng_ref, plnb_ref, mod_ref, w32_ref = refs[:8]
        refs = refs[8:]
    else:
        x_ref, mod_ref, w32_ref = refs[:3]
        refs = refs[3:]
    q_ref, k_ref, v_ref, z_ref = refs[:4]
    refs = refs[4:]
    if emit_state:
        sk_ref, sv_ref = refs[:2]
        refs = refs[2:]
    if fuse_prev:
        y_ref, w_ref, pwo_ref = refs
        _cast_weights_once(2, [(w32_ref, w_ref), (pwo32_ref, pwo_ref)])
        x = _out_epilogue(a_ref, x_ref, pmod_ref, pwo_ref, plng_ref, plnb_ref)
        y_ref[0] = x
    else:
        (w_ref,) = refs
        _cast_weights_once(2, [(w32_ref, w_ref)])
        x = x_ref[0]
    shift, scale, _ = _modulation(mod_ref)
    h = (x * (1.0 + scale) + shift).astype(BF16)
    w = NA_WIDTH
    q_ref[0] = (_dot(h, w_ref[:, :w]) * NA_Q_SCALE).astype(BF16)
    k = _dot(h, w_ref[:, w:2 * w])
    v = _dot(h, w_ref[:, 2 * w:3 * w])
    k_ref[0] = k.astype(BF16)
    v_ref[0] = v.astype(BF16)
    z_ref[0] = _dot(h, w_ref[:, 3 * w:])
    if emit_state:
        sk_ref[...] = k.reshape(sk_ref.shape)
        sv_ref[...] = v.reshape(sv_ref.shape)


def _na_proj(x, mod, slot, w_in, emit_state, prev=None):
    bsz, seq_len, _ = x.shape
    t_rows = min(ROW_TILE, seq_len)
    const2 = lambda b, i: (0, 0)
    tok = pl.BlockSpec((1, t_rows, NA_WIDTH), lambda b, i: (b, i, 0))
    mod_spec = pl.BlockSpec((1, 1, 3 * D_MODEL), lambda b, i: (b, 0, 0))
    w_spec = _weight_spec((1, D_MODEL, 4 * NA_WIDTH), (slot, 0, 0))
    out_specs = [tok, tok, tok, tok]
    act = jax.ShapeDtypeStruct((bsz, seq_len, NA_WIDTH), BF16)
    out_shape = [act, act, act, jax.ShapeDtypeStruct((bsz, seq_len, NA_WIDTH), F32)]
    scratch = [pltpu.VMEM((D_MODEL, 4 * NA_WIDTH), BF16)]
    if emit_state:
        st_spec, st_shape = _state_out(bsz, seq_len, t_rows, emit_state)
        out_specs += [st_spec(NA_WIDTH)] * 2
        out_shape += [st_shape(NA_WIDTH)] * 2
    if prev is None:
        in_specs = [tok, mod_spec, w_spec]
        args = [x, mod, w_in]
    else:
        a, mod_prev, slot_prev, w_out_prev, ln_g_prev, ln_b_prev = prev
        in_specs = [tok, tok, mod_spec, _weight_spec((1, D_MODEL, D_MODEL), (slot_prev, 0, 0)),
                    pl.BlockSpec((1, D_MODEL), const2), pl.BlockSpec((1, D_MODEL), const2), mod_spec, w_spec]
        args = [a, x, mod_prev, w_out_prev, ln_g_prev, ln_b_prev, mod, w_in]
        out_specs.append(tok)
        out_shape.append(jax.ShapeDtypeStruct(x.shape, F32))
        scratch.append(pltpu.VMEM((D_MODEL, D_MODEL), BF16))
    return pl.pallas_call(
        functools.partial(_na_proj_kernel, emit_state, prev is not None),
        grid=(bsz, seq_len // t_rows),
        in_specs=in_specs,
        out_specs=out_specs,
        out_shape=out_shape,
        scratch_shapes=scratch,
        compiler_params=_params(2),
        name="na_proj",
    )(*args)


def _split_head_pair(q2, low_half):
    zero = jnp.zeros_like(q2)
    return jnp.concatenate([jnp.where(low_half, q2, zero), jnp.where(low_half, zero, q2)], axis=0)


def _na_ctx_attn_kernel(q_ref, k_ref, v_ref, z_ref, o_ref):
    n = q_ref.shape[1]
    low_half = lax.broadcasted_iota(jnp.int32, (1, LANES), 1) < NA_HEAD_DIM
    for j in range(NA_PAIRS):
        lanes = slice(j * LANES, (j + 1) * LANES)
        qs = _split_head_pair(q_ref[0, :, lanes], low_half)
        o = _softmax_pv([_dot_nt(qs, k_ref[0, :, lanes])], [v_ref[0, :, lanes]])
        o2 = jnp.where(low_half, o[:n], o[n:])
        o_ref[0, :, lanes] = (o2 * _silu(z_ref[0, :, lanes])).astype(BF16)


def _na_ctx_kernel(q_ref, k_ref, v_ref, z_ref, x_ref, mod_ref, wo32_ref, lng_ref, lnb_ref, y_ref, a_ref, wo_ref):
    _cast_weights_once(1, [(wo32_ref, wo_ref)])
    _na_ctx_attn_kernel(q_ref, k_ref, v_ref, z_ref, a_ref)
    y_ref[0] = _out_epilogue(a_ref, x_ref, mod_ref, wo_ref, lng_ref, lnb_ref)


def _na_ctx_layer(q, k, v, z, x, mod, slot, w_out, ln_g, ln_b):
    bsz, seq_len, _ = x.shape
    tok, tail_specs = _ctx_fused_specs(seq_len, slot)
    return pl.pallas_call(
        _na_ctx_kernel,
        grid=(bsz,),
        in_specs=[tok, tok, tok, tok] + tail_specs,
        out_specs=tok,
        out_shape=jax.ShapeDtypeStruct(x.shape, F32),
        scratch_shapes=[pltpu.VMEM((1, seq_len, NA_WIDTH), BF16), pltpu.VMEM((D_MODEL, D_MODEL), BF16)],
        compiler_params=_params(1),
        name="na_ctx_layer",
    )(q, k, v, z, x, mod, w_out, ln_g, ln_b)


NA_N_DR = 2 * NA_WIN_R - 1
NA_BIAS_SLOTS = NA_N_DR + 1


def _na_bias_kernel(rpb_ref, o_ref):
    qc = lax.broadcasted_iota(jnp.int32, (GRID_W, LANES), 0)
    lane = lax.broadcasted_iota(jnp.int32, (GRID_W, LANES), 1)
    low_half = lane < GRID_W
    kc = jnp.where(low_half, lane, lane - GRID_W)
    cstart = jnp.clip(qc - NA_WIN_C // 2, 0, GRID_W - NA_WIN_C)
    valid = (kc >= cstart) & (kc < cstart + NA_WIN_C)
    for i in range(NA_BIAS_SLOTS):
        row_l = jnp.broadcast_to(rpb_ref[0, pl.ds(max(i - 1, 0), 1), :], (GRID_W, LANES))
        row_r = jnp.broadcast_to(rpb_ref[0, pl.ds(min(i, NA_N_DR - 1), 1), :], (GRID_W, LANES))
        left = pltpu.roll(row_l, LANES - (NA_WIN_C - 1), 1, stride=1, stride_axis=0)
        right = pltpu.roll(row_r, GRID_W - (NA_WIN_C - 1), 1, stride=1, stride_axis=0)
        tile = jnp.where(low_half, left, right)
        o_ref[0, i] = jnp.where(valid, tile * LOG2E, NEG_INF)


def _na_bias_table(rpb):
    n_dr, n_dc = rpb.shape[1:]
    rpb_rows = jnp.pad(rpb, ((0, 0), (0, 16 - n_dr), (0, LANES - n_dc)))
    return pl.pallas_call(
        _na_bias_kernel,
        grid=(NA_HEADS,),
        in_specs=[pl.BlockSpec((1, 16, LANES), lambda h: (h, 0, 0))],
        out_specs=pl.BlockSpec((1, NA_BIAS_SLOTS, GRID_W, LANES), lambda h: (h, 0, 0, 0)),
        out_shape=jax.ShapeDtypeStruct((NA_HEADS, NA_BIAS_SLOTS, GRID_W, LANES), F32),
        compiler_params=_params(1),
        name="na_bias_table",
    )(rpb_rows)


NA_ROWS_PER_GROUP = 4
NA_BAND_ROWS = NA_ROWS_PER_GROUP + NA_WIN_R
NA_GROUPS_PER_STEP = 8


def _na_lat_attn_kernel(n_rows, n_groups, q_ref, k_ref, v_ref, kc_ref, vc_ref, b_ref, z_ref, o_ref):
    n_q = NA_ROWS_PER_GROUP * GRID_W
    low_half = lax.broadcasted_iota(jnp.int32, (1, LANES), 1) < NA_HEAD_DIM
    for grp in range(n_groups):
        rows = slice(grp * n_q, (grp + 1) * n_q)
        r0 = (pl.program_id(2) * n_groups + grp) * NA_ROWS_PER_GROUP
        band0 = jnp.clip(r0 - NA_WIN_R // 2, 0, n_rows - NA_BAND_ROWS)
        qs = _split_head_pair(q_ref[0, rows, :], low_half)
        key0 = pl.multiple_of(band0 * GRID_W, GRID_W)
        kl = k_ref[0, pl.ds(key0, NA_BAND_ROWS * GRID_W), :]
        vl = v_ref[0, pl.ds(key0, NA_BAND_ROWS * GRID_W), :]
        bias_rows = []
        for rr in range(NA_ROWS_PER_GROUP):
            r = r0 + rr
            win0 = jnp.clip(r - NA_WIN_R // 2, 0, n_rows - NA_WIN_R)
            blocks = []
            for cblk in range(NA_BAND_ROWS // 2):
                key_row = band0 + 2 * cblk
                slot = jnp.clip(key_row - r + NA_WIN_R, 0, NA_BIAS_SLOTS - 1)
                left_pen = jnp.where((key_row >= win0) & (key_row < win0 + NA_WIN_R), 0.0, NEG_INF)
                right_pen = jnp.where((key_row + 1 >= win0) & (key_row + 1 < win0 + NA_WIN_R), 0.0, NEG_INF)
                blocks.append((slot, jnp.where(low_half, left_pen, right_pen)))
            bias_rows.append(blocks)
        bias = jnp.concatenate(
            [jnp.concatenate(
                [jnp.concatenate([b_ref[e, slot] + pen for slot, pen in blocks], axis=1)
                 for blocks in bias_rows], axis=0)
             for e in range(2)], axis=0)
        o = _softmax_pv([_dot_nt(qs, kl) + bias, _dot_nt(qs, kc_ref[0])], [vl, vc_ref[0]])
        o2 = jnp.where(low_half, o[:n_q], o[n_q:])
        o_ref[0, rows, :] = (o2 * _silu(z_ref[0, rows, :])).astype(BF16)


def _na_lat_attention(q, k, v, k_ctx, v_ctx, bias, z):
    bsz, seq_len, _ = q.shape
    n_rows = seq_len // GRID_W
    past = k_ctx.shape[1]
    n_groups = min(NA_GROUPS_PER_STEP, n_rows // NA_ROWS_PER_GROUP)
    rows_per_step = n_groups * NA_ROWS_PER_GROUP
    row_blk = pl.BlockSpec((1, rows_per_step * GRID_W, LANES), lambda b, j, r: (b, r, j))
    seq_blk = pl.BlockSpec((1, seq_len, LANES), lambda b, j, r: (b, 0, j))
    ctx_blk = pl.BlockSpec((1, past, LANES), lambda b, j, r: (b, 0, j))
    return pl.pallas_call(
        functools.partial(_na_lat_attn_kernel, n_rows, n_groups),
        grid=(bsz, NA_PAIRS, n_rows // rows_per_step),
        in_specs=[row_blk, seq_blk, seq_blk, ctx_blk, ctx_blk,
                  pl.BlockSpec((2, NA_BIAS_SLOTS, GRID_W, LANES), lambda b, j, r: (j, 0, 0, 0)),
                  row_blk],
        out_specs=row_blk,
        out_shape=jax.ShapeDtypeStruct((bsz, seq_len, NA_WIDTH), BF16),
        compiler_params=_params(3),
        name="na_lat_attention",
    )(q, k, v, k_ctx, v_ctx, bias, z)


def _rope_swap_perm():
    nf = QK_ROPE // 4
    idx = np.arange(QK_ROPE).reshape(2, 2, nf)
    return idx[:, ::-1, :].reshape(-1)


def _rope_table(seq_len):
    nf = QK_ROPE // 4
    inv = ROPE_BASE ** (-jnp.arange(nf, dtype=F32) / nf)
    t = jnp.arange(seq_len)
    pos = jnp.stack([t // GRID_W, t % GRID_W], -1).astype(F32)
    ang = pos[:, :, None] * inv
    cos = jnp.cos(ang)
    sin = jnp.sin(ang)
    c = jnp.concatenate([cos, cos], axis=-1).reshape(seq_len, QK_ROPE)
    s = jnp.concatenate([-sin, sin], axis=-1).reshape(seq_len, QK_ROPE)
    return jnp.concatenate([c, s], axis=-1)


def _mla_weights(w_in, w_uq):
    perm = _rope_swap_perm()
    o1, o2, o3 = Q_LORA, Q_LORA + KV_LORA, Q_LORA + KV_LORA + QK_ROPE
    w_kr = w_in[:, o2:o3]
    w1 = jnp.concatenate([w_in[:, :o2], w_kr, w_kr[:, perm], w_in[:, o3:]], axis=1).astype(BF16)
    wq = w_uq.reshape(Q_LORA, MLA_HEADS, QK_NOPE + QK_ROPE)
    nope = wq[:, :, :QK_NOPE].reshape(Q_LORA, MLA_HEADS * QK_NOPE)
    rope = wq[:, :, QK_NOPE:]
    rope_blk = jnp.concatenate([rope, rope[:, :, perm]], axis=-1).reshape(Q_LORA, MLA_HEADS * LANES)
    return w1, jnp.concatenate([nope, rope_blk], axis=1).astype(BF16)


def kernel(x_prompt, x_sample, cache_mla_ckv, cache_mla_krope, cache_na_k, cache_na_v, c, c_ctx, ada_w, ada_b, ln_g, ln_b, pool_w_in, pool_w_grp, pool_scale, pool_w_out, mla_w_in, mla_q_norm, mla_w_uq, mla_kv_norm, mla_w_ukv, mla_w_out, na_w_in, na_rpb, na_w_out):
    n_ctx, n_lat = x_prompt.shape[0], x_sample.shape[0]
    lat_len = x_sample.shape[1]
    cond = jnp.concatenate([c_ctx[None], c, jnp.zeros((8 - 1 - n_lat, D_MODEL), F32)], axis=0)
    mod = _ada(cond, ada_w, ada_b)
    yp, ys = x_prompt, x_sample
    st_ckv = st_kr = st_k = st_v = None
    ctx_len = x_prompt.shape[1]
    ctx_fold = 2 if (n_ctx % 2 == 0 and 2 * ctx_len <= ROW_TILE) else 1
    fold = lambda t: t.reshape((n_ctx // ctx_fold, ctx_fold * ctx_len) + t.shape[2:])
    unfold = lambda t: t.reshape((n_ctx, ctx_len) + t.shape[2:])
    pending = None
    for i in range(DEPTH):
        kind, j = i % N_MIXERS, i // N_MIXERS
        mod_p = jnp.broadcast_to(mod[i, 0:1][None], (n_ctx, 1, 3 * D_MODEL))
        mod_pf = mod_p[:n_ctx // ctx_fold]
        mod_s = mod[i, 1:1 + n_lat][:, None, :]
        g, b = ln_g[i][None], ln_b[i][None]
        if kind != 2 and pending is not None:
            ys = _out_layer(pending[0], ys, *pending[1:])
            pending = None
        if kind == 0:
            ps = pool_scale[j][None]
            yp = _pool_layer(yp, mod_p, j, pool_w_in, pool_w_grp, ps, pool_w_out, g, b)
            ys = _pool_layer(ys, mod_s, j, pool_w_in, pool_w_grp, ps, pool_w_out, g, b)
        elif kind == 1:
            w1, w_uq = _mla_weights(mla_w_in[j], mla_w_uq[j])
            w_ukv = mla_w_ukv[j].astype(BF16)
            qn, kvn = mla_q_norm[j][None], mla_kv_norm[j][None]
            q_p, kv_p, krp_p, z_p, st_ckv, st_kr = _mla_proj(fold(yp), mod_pf, w1, qn, kvn, w_uq, w_ukv, None, ctx_fold)
            yp = _mla_ctx_layer(unfold(q_p), unfold(z_p), unfold(kv_p), unfold(krp_p), yp, mod_p, j, mla_w_out, g, b)
            q_s, kv_s, krp_s, z_s = _mla_proj(ys, mod_s, w1, qn, kvn, w_uq, w_ukv, _rope_table(lat_len), False)
            kv_c, krp_c = _mla_cache_keys(cache_mla_ckv, cache_mla_krope, j, w_ukv)
            a_s = _mla_attention(q_s, z_s, [(kv_c, krp_c), (kv_s, krp_s)], lat_len, 2, 2 * ATTN_ROWS)
            pending = (a_s, mod_s, j, mla_w_out, g, b)
        else:
            q_p, k_p, v_p, z_p, st_k, st_v = _na_proj(fold(yp), mod_pf, j, na_w_in, ctx_fold)
            yp = _na_ctx_layer(unfold(q_p), unfold(k_p), unfold(v_p), unfold(z_p), yp, mod_p, j, na_w_out, g, b)
            if pending is None:
                q_s, k_s, v_s, z_s = _na_proj(ys, mod_s, j, na_w_in, False)
            else:
                q_s, k_s, v_s, z_s, ys = _na_proj(ys, mod_s, j, na_w_in, False, prev=pending)
                pending = None
            past = cache_na_k.shape[2]
            k_c = cache_na_k[:, j].reshape(n_lat, past, NA_WIDTH).astype(BF16)
            v_c = cache_na_v[:, j].reshape(n_lat, past, NA_WIDTH).astype(BF16)
            bias = _na_bias_table(na_rpb[j])
            a_s = _na_lat_attention(q_s, k_s, v_s, k_c, v_c, bias, z_s)
            pending = (a_s, mod_s, j, na_w_out, g, b)
    if pending is not None:
        ys = _out_layer(pending[0], ys, *pending[1:])
    state_na_k = st_k.reshape(n_ctx, 1, -1, NA_HEADS, NA_HEAD_DIM)
    state_na_v = st_v.reshape(n_ctx, 1, -1, NA_HEADS, NA_HEAD_DIM)
    return (yp, ys, st_ckv, st_kr, state_na_k, state_na_v)
```

```python
import functools
import math

import numpy as np
import jax
import jax.numpy as jnp
from jax import lax
from jax.experimental import pallas as pl
from jax.experimental.pallas import tpu as pltpu

F32 = jnp.float32
BF16 = jnp.bfloat16

D_MODEL = 1024
DEPTH = 4
GRID_W = 64
N_MIXERS = 3

POOL_WINDOWS = (2, 4, 8, 16)
POOL_GROUP_DIM = D_MODEL // len(POOL_WINDOWS)
POOL_HALO = 8

MLA_HEADS = 8
QK_NOPE = 128
QK_ROPE = 64
V_HEAD = 128
Q_LORA = D_MODEL // 2
KV_LORA = D_MODEL // 4
MLA_WIDTH = MLA_HEADS * V_HEAD
MLA_SCALE = (QK_NOPE + QK_ROPE) ** -0.5

NA_HEADS = 16
NA_HEAD_DIM = 64
NA_WIDTH = NA_HEADS * NA_HEAD_DIM
NA_WIN_R = 8
NA_WIN_C = 16
NA_SCALE = NA_HEAD_DIM ** -0.5
NA_PAIRS = NA_HEADS // 2

ROPE_BASE = 10000.0
LN_EPS = 1e-5
RMS_EPS = 1e-6
NEG_INF = -1e30
DEEPNORM_ALPHA = (2 * DEPTH) ** 0.25
LOG2E = math.log2(math.e)

LANES = 128
ROW_TILE = 512
ATTN_ROWS = 256
VMEM_LIMIT = 56 * 1024 * 1024


def _params(n_axes):
    return pltpu.CompilerParams(dimension_semantics=("arbitrary",) * n_axes, vmem_limit_bytes=VMEM_LIMIT)


def _silu(x):
    h = 0.5 * x
    return h + h * jnp.tanh(h)


def _dot(a, b):
    return jnp.dot(a, b, preferred_element_type=F32)


def _dot_nt(a, b):
    return lax.dot_general(a, b, (((1,), (1,)), ((), ())), preferred_element_type=F32)


def _weight_spec(block_shape, index):
    return pl.BlockSpec(block_shape, lambda *_: index, pipeline_mode=pl.Buffered(1))


def _cast_weights_once(n_axes, pairs):
    first = functools.reduce(jnp.logical_and, [pl.program_id(a) == 0 for a in range(n_axes)])

    @pl.when(first)
    def _():
        for src, dst in pairs:
            dst[...] = src[...].reshape(dst.shape).astype(BF16)


def _state_out(bsz, seq_len, t_rows, n_seq):
    assert t_rows == seq_len and seq_len % n_seq == 0
    spec = lambda width: pl.BlockSpec((n_seq, 1, seq_len // n_seq, width), lambda b, i: (b, 0, 0, 0))
    shape = lambda width: jax.ShapeDtypeStruct((bsz * n_seq, 1, seq_len // n_seq, width), F32)
    return spec, shape


def _modulation(mod_ref):
    m = mod_ref[0]
    return m[:, :D_MODEL], m[:, D_MODEL:2 * D_MODEL], m[:, 2 * D_MODEL:]


def _deepnorm_ln(x, gate, branch, g, b):
    xf = DEEPNORM_ALPHA * x + gate * branch
    mu = jnp.mean(xf, axis=-1, keepdims=True)
    xc = xf - mu
    var = jnp.mean(xc * xc, axis=-1, keepdims=True)
    return xc * lax.rsqrt(var + LN_EPS) * g + b


def _rms(x, g):
    return x * lax.rsqrt(jnp.mean(x * x, axis=-1, keepdims=True) + RMS_EPS) * g


def _ada_kernel(cond_ref, w_ref, b_ref, o_ref):
    c = cond_ref[...]
    o_ref[0] = _dot(_silu(c).astype(BF16), w_ref[0].astype(BF16)) + b_ref[0]


def _ada(cond, ada_w, ada_b):
    tn = D_MODEL
    return pl.pallas_call(
        _ada_kernel,
        grid=(DEPTH, 3 * D_MODEL // tn),
        in_specs=[
            pl.BlockSpec((8, D_MODEL), lambda l, n: (0, 0)),
            pl.BlockSpec((1, D_MODEL, tn), lambda l, n: (l, 0, n)),
            pl.BlockSpec((1, 1, tn), lambda l, n: (l, 0, n)),
        ],
        out_specs=pl.BlockSpec((1, 8, tn), lambda l, n: (l, 0, n)),
        out_shape=jax.ShapeDtypeStruct((DEPTH, 8, 3 * D_MODEL), F32),
        compiler_params=_params(2),
        name="ada",
    )(cond, ada_w, ada_b.reshape(DEPTH, 1, 3 * D_MODEL))


def _rows_ahead(x, k):
    return x if k == 0 else pltpu.roll(x, x.shape[0] - k, 0)


POOL_CHAIN_ROWS = 256


def _pool_chain(seq_len, t0, x, x_prev, x_next, mods, weights, ln):
    c_rows = x.shape[0]
    shift, scale, gate = mods
    wu_ref, wz_ref, wg_ref, ps_ref, wo_ref = weights
    one_scale = 1.0 + scale
    hm = x * one_scale + shift
    hp = x_prev * one_scale + shift
    hn = x_next * one_scale + shift
    hext = jnp.concatenate([hm, hp, hn], axis=0).astype(BF16)
    u = _dot(hext, wu_ref[...])
    z = _dot(hext[:c_rows], wz_ref[...])
    prev_ok = (t0 > 0).astype(F32)
    next_ok = (t0 + c_rows < seq_len).astype(F32)
    centre = u[:c_rows]
    useq = jnp.concatenate([u[c_rows:c_rows + POOL_HALO] * prev_ok, centre, u[c_rows + POOL_HALO:] * next_ok], axis=0)

    t = t0 + lax.broadcasted_iota(jnp.int32, (c_rows, 1), 0)
    mixed = []
    for g, w in enumerate(POOL_WINDOWS):
        cols = slice(g * POOL_GROUP_DIM, (g + 1) * POOL_GROUP_DIM)
        q = useq[:, cols]
        span = 1
        while span < w // 2:
            q = q + _rows_ahead(q, span)
            span *= 2
        acc = _rows_ahead(q, POOL_HALO - w // 2)[:c_rows] + q[POOL_HALO:POOL_HALO + c_rows]
        lo = jnp.clip(t - w // 2, 0, seq_len)
        hi = jnp.clip(t + (w - w // 2), 0, seq_len)
        inv_cnt = 1.0 / (hi - lo).astype(F32)
        mg = (acc * inv_cnt - centre[:, cols]).astype(BF16)
        mixed.append(_dot(mg, wg_ref[g]))
    mixed = jnp.concatenate(mixed, axis=1) * ps_ref[...]
    a = (mixed * _silu(z)).astype(BF16)
    branch = _dot(a, wo_ref[...])
    return _deepnorm_ln(x, gate, branch, *ln)


def _pool_kernel(seq_len, x_ref, xp_ref, xn_ref, mod_ref, wu32_ref, wz32_ref, wg32_ref, ps_ref, wo32_ref,
                 lng_ref, lnb_ref, o_ref, wu_ref, wz_ref, wg_ref, wo_ref):
    _cast_weights_once(2, [(wu32_ref, wu_ref), (wz32_ref, wz_ref), (wg32_ref, wg_ref), (wo32_ref, wo_ref)])
    t_rows = x_ref.shape[1]
    c_rows = min(POOL_CHAIN_ROWS, t_rows)
    n_chain = t_rows // c_rows
    mods = _modulation(mod_ref)
    weights = (wu_ref, wz_ref, wg_ref, ps_ref, wo_ref)
    ln = (lng_ref[...], lnb_ref[...])
    for ci in range(n_chain):
        r0 = ci * c_rows
        t0 = lax.rem(pl.program_id(1) * t_rows + r0, seq_len)
        x_prev = xp_ref[0] if ci == 0 else x_ref[0, r0 - POOL_HALO:r0, :]
        x_next = xn_ref[0] if ci == n_chain - 1 else x_ref[0, r0 + c_rows:r0 + c_rows + POOL_HALO, :]
        o_ref[0, r0:r0 + c_rows, :] = _pool_chain(seq_len, t0, x_ref[0, r0:r0 + c_rows, :], x_prev, x_next,
                                                  mods, weights, ln)


def _pool_layer(x, seq_len, mod, slot, w_in, w_grp, p_scale, w_out, ln_g, ln_b):
    bsz, row_len, _ = x.shape
    n_grp = len(POOL_WINDOWS)
    t_rows = min(ROW_TILE, row_len)
    assert row_len % seq_len == 0 and (seq_len % t_rows == 0 or t_rows % seq_len == 0)
    assert min(POOL_CHAIN_ROWS, t_rows) <= seq_len
    n_tiles = row_len // t_rows
    halo_per_tile = t_rows // POOL_HALO
    halo_per_seq = row_len // POOL_HALO
    n_halo = bsz * halo_per_seq
    xh = x.reshape(n_halo, POOL_HALO, D_MODEL)
    const2 = lambda b, i: (0, 0)
    return pl.pallas_call(
        functools.partial(_pool_kernel, seq_len),
        grid=(bsz, n_tiles),
        in_specs=[
            pl.BlockSpec((1, t_rows, D_MODEL), lambda b, i: (b, i, 0)),
            pl.BlockSpec((1, POOL_HALO, D_MODEL),
                         lambda b, i: (jnp.maximum(b * halo_per_seq + i * halo_per_tile - 1, 0), 0, 0)),
            pl.BlockSpec((1, POOL_HALO, D_MODEL),
                         lambda b, i: (jnp.minimum(b * halo_per_seq + (i + 1) * halo_per_tile, n_halo - 1), 0, 0)),
            pl.BlockSpec((1, 1, 3 * D_MODEL), lambda b, i: (b, 0, 0)),
            _weight_spec((1, D_MODEL, D_MODEL), (slot, 0, 0)),
            _weight_spec((1, D_MODEL, D_MODEL), (slot, 0, 1)),
            _weight_spec((1, n_grp, POOL_GROUP_DIM, POOL_GROUP_DIM), (slot, 0, 0, 0)),
            pl.BlockSpec((1, D_MODEL), const2),
            _weight_spec((1, D_MODEL, D_MODEL), (slot, 0, 0)),
            pl.BlockSpec((1, D_MODEL), const2),
            pl.BlockSpec((1, D_MODEL), const2),
        ],
        out_specs=pl.BlockSpec((1, t_rows, D_MODEL), lambda b, i: (b, i, 0)),
        out_shape=jax.ShapeDtypeStruct(x.shape, F32),
        scratch_shapes=[pltpu.VMEM((D_MODEL, D_MODEL), BF16),
                        pltpu.VMEM((D_MODEL, D_MODEL), BF16),
                        pltpu.VMEM((n_grp, POOL_GROUP_DIM, POOL_GROUP_DIM), BF16),
                        pltpu.VMEM((D_MODEL, D_MODEL), BF16)],
        compiler_params=_params(2),
        name="pool_layer",
    )(x, xh, xh, mod, w_in, w_in, w_grp, p_scale, w_out, ln_g, ln_b)


def _out_epilogue(a_ref, x_ref, mod_ref, wo_ref, lng_ref, lnb_ref):
    _, _, gate = _modulation(mod_ref)
    branch = _dot(a_ref[0], wo_ref[...])
    return _deepnorm_ln(x_ref[0], gate, branch, lng_ref[...], lnb_ref[...])


def _out_kernel(a_ref, x_ref, mod_ref, wo32_ref, lng_ref, lnb_ref, o_ref, wo_ref):
    _cast_weights_once(2, [(wo32_ref, wo_ref)])
    o_ref[0] = _out_epilogue(a_ref, x_ref, mod_ref, wo_ref, lng_ref, lnb_ref)


def _out_layer(a, x, mod, slot, w_out, ln_g, ln_b):
    bsz, seq_len, _ = x.shape
    t_rows = min(ROW_TILE, seq_len)
    const2 = lambda b, i: (0, 0)
    tok = pl.BlockSpec((1, t_rows, D_MODEL), lambda b, i: (b, i, 0))
    return pl.pallas_call(
        _out_kernel,
        grid=(bsz, seq_len // t_rows),
        in_specs=[tok, tok,
                  pl.BlockSpec((1, 1, 3 * D_MODEL), lambda b, i: (b, 0, 0)),
                  _weight_spec((1, D_MODEL, D_MODEL), (slot, 0, 0)),
                  pl.BlockSpec((1, D_MODEL), const2),
                  pl.BlockSpec((1, D_MODEL), const2)],
        out_specs=tok,
        out_shape=jax.ShapeDtypeStruct(x.shape, F32),
        scratch_shapes=[pltpu.VMEM((D_MODEL, D_MODEL), BF16)],
        compiler_params=_params(2),
        name="out_layer",
    )(a, x, mod, w_out, ln_g, ln_b)


MLA_W1_COLS = Q_LORA + KV_LORA + 2 * QK_ROPE + MLA_WIDTH
MLA_Z_OFF = Q_LORA + KV_LORA + 2 * QK_ROPE
MLA_Q_COLS = MLA_HEADS * QK_NOPE + MLA_HEADS * LANES
MLA_Q_SCALE = MLA_SCALE * LOG2E


def _rope_block(blk, cs, low_half):
    if cs is not None:
        blk = blk * cs
        blk = blk + pltpu.roll(blk, LANES // 2, 1)
    return jnp.where(low_half, blk, 0.0)


def _mla_proj_kernel(rope, emit_state, *refs):
    x_ref, mod_ref, w1_ref, qn_ref, kvn_ref, wuq_ref, wukv_ref = refs[:7]
    refs = refs[7:]
    cs = None
    if rope:
        cs = refs[0][...]
        refs = refs[1:]
    q_ref, kv_ref, krp_ref, z_ref = refs[:4]
    shift, scale, _ = _modulation(mod_ref)
    h = (x_ref[0] * (1.0 + scale) + shift).astype(BF16)
    p = _dot(h, w1_ref[...])
    cq = p[:, :Q_LORA]
    ckv = p[:, Q_LORA:Q_LORA + KV_LORA]
    krb = p[:, Q_LORA + KV_LORA:MLA_Z_OFF]
    z_ref[0] = p[:, MLA_Z_OFF:]
    ckvn = _rms(ckv, kvn_ref[...])
    if emit_state:
        ckv_ref, kr_ref = refs[4:6]
        ckv_ref[...] = ckvn.reshape(ckv_ref.shape)
        kr_ref[...] = krb[:, :QK_ROPE].reshape(kr_ref.shape)
    low_half = lax.broadcasted_iota(jnp.int32, (1, LANES), 1) < LANES // 2
    krp_ref[0] = _rope_block(krb, cs, low_half).astype(BF16)
    kv_ref[0] = _dot(ckvn.astype(BF16), wukv_ref[...]).astype(BF16)
    qf = _dot(_rms(cq, qn_ref[...]).astype(BF16), wuq_ref[...])
    n_nope = MLA_HEADS * QK_NOPE
    q_ref[0, :, :n_nope] = (qf[:, :n_nope] * MLA_Q_SCALE).astype(BF16)
    for hh in range(MLA_HEADS):
        cols = slice(n_nope + hh * LANES, n_nope + (hh + 1) * LANES)
        q_ref[0, :, cols] = (_rope_block(qf[:, cols], cs, low_half) * MLA_Q_SCALE).astype(BF16)


def _mla_proj(x, mod, w1, q_norm, kv_norm, w_uq, w_ukv, rope_cs, emit_state):
    bsz, seq_len, _ = x.shape
    t_rows = min(ROW_TILE, seq_len)
    const2 = lambda b, i: (0, 0)
    tok = lambda width: pl.BlockSpec((1, t_rows, width), lambda b, i: (b, i, 0))
    in_specs = [tok(D_MODEL),
                pl.BlockSpec((1, 1, 3 * D_MODEL), lambda b, i: (b, 0, 0)),
                pl.BlockSpec((D_MODEL, MLA_W1_COLS), const2),
                pl.BlockSpec((1, Q_LORA), const2),
                pl.BlockSpec((1, KV_LORA), const2),
                pl.BlockSpec((Q_LORA, MLA_Q_COLS), const2),
                pl.BlockSpec((KV_LORA, MLA_HEADS * (QK_NOPE + V_HEAD)), const2)]
    args = [x, mod, w1, q_norm, kv_norm, w_uq, w_ukv]
    if rope_cs is not None:
        in_specs.append(pl.BlockSpec((t_rows, LANES), lambda b, i: (i, 0)))
        args.append(rope_cs)
    out_specs = [tok(MLA_Q_COLS), tok(MLA_HEADS * (QK_NOPE + V_HEAD)), tok(LANES), tok(MLA_WIDTH)]
    out_shape = [jax.ShapeDtypeStruct((bsz, seq_len, MLA_Q_COLS), BF16),
                 jax.ShapeDtypeStruct((bsz, seq_len, MLA_HEADS * (QK_NOPE + V_HEAD)), BF16),
                 jax.ShapeDtypeStruct((bsz, seq_len, LANES), BF16),
                 jax.ShapeDtypeStruct((bsz, seq_len, MLA_WIDTH), F32)]
    if emit_state:
        st_spec, st_shape = _state_out(bsz, seq_len, t_rows, emit_state)
        out_specs += [st_spec(KV_LORA), st_spec(QK_ROPE)]
        out_shape += [st_shape(KV_LORA), st_shape(QK_ROPE)]
    return pl.pallas_call(
        functools.partial(_mla_proj_kernel, rope_cs is not None, emit_state),
        grid=(bsz, seq_len // t_rows),
        in_specs=in_specs,
        out_specs=out_specs,
        out_shape=out_shape,
        compiler_params=_params(2),
        name="mla_proj",
    )(*args)


def _mla_cache_kernel(ckv_ref, kr_ref, wukv_ref, kv_ref, krp_ref):
    kv_ref[0] = _dot(ckv_ref[0, 0].astype(BF16), wukv_ref[...]).astype(BF16)
    krp_ref[0, :, :QK_ROPE] = kr_ref[0, 0].astype(BF16)
    krp_ref[0, :, QK_ROPE:] = jnp.zeros((kr_ref.shape[2], LANES - QK_ROPE), BF16)


def _mla_cache_keys(cache_ckv, cache_kr, slot, w_ukv):
    bsz, _, past, _ = cache_ckv.shape
    kv_cols = MLA_HEADS * (QK_NOPE + V_HEAD)
    return pl.pallas_call(
        _mla_cache_kernel,
        grid=(bsz,),
        in_specs=[pl.BlockSpec((1, 1, past, KV_LORA), lambda b: (b, slot, 0, 0)),
                  pl.BlockSpec((1, 1, past, QK_ROPE), lambda b: (b, slot, 0, 0)),
                  pl.BlockSpec((KV_LORA, kv_cols), lambda b: (0, 0))],
        out_specs=[pl.BlockSpec((1, past, kv_cols), lambda b: (b, 0, 0)),
                   pl.BlockSpec((1, past, LANES), lambda b: (b, 0, 0))],
        out_shape=[jax.ShapeDtypeStruct((bsz, past, kv_cols), BF16),
                   jax.ShapeDtypeStruct((bsz, past, LANES), BF16)],
        compiler_params=_params(1),
        name="mla_cache_keys",
    )(cache_ckv, cache_kr, w_ukv)


def _softmax_pv(scores, values):
    m = functools.reduce(jnp.maximum, [jnp.max(s, axis=1, keepdims=True) for s in scores])
    denom = 0.0
    acc = 0.0
    for s, v in zip(scores, values):
        p = jnp.exp2(s - m)
        denom = denom + jnp.sum(p, axis=1, keepdims=True)
        acc = acc + _dot(p.astype(BF16), v)
    return acc / denom


def _mla_attn_kernel(n_seg, heads, sub_rows, qn_ref, qr_ref, z_ref, *refs):
    o_ref = refs[2 * n_seg]
    q_rows = qn_ref.shape[1]
    kv_w = QK_NOPE + V_HEAD
    for h in range(heads):
        lanes = slice(h * LANES, (h + 1) * LANES)
        keys, values = [], []
        for s in range(n_seg):
            kv_ref, kr_ref = refs[2 * s:2 * s + 2]
            keys.append(jnp.concatenate([kv_ref[0, :, h * kv_w:h * kv_w + QK_NOPE], kr_ref[0]], axis=1))
            values.append(kv_ref[0, :, h * kv_w + QK_NOPE:(h + 1) * kv_w])
        for r0 in range(0, q_rows, sub_rows):
            rows = slice(r0, r0 + sub_rows)
            q = jnp.concatenate([qn_ref[0, rows, lanes], qr_ref[0, rows, lanes]], axis=1)
            o = _softmax_pv([_dot_nt(q, k) for k in keys], values)
            o_ref[0, rows, lanes] = (o * _silu(z_ref[0, rows, lanes])).astype(BF16)


def _mla_attention(q, z, segments, q_tile, heads, sub_rows):
    bsz, lq, _ = q.shape
    n_hblk = MLA_HEADS // heads
    hw = heads * LANES
    in_specs = [pl.BlockSpec((1, q_tile, hw), lambda b, h, i: (b, i, h)),
                pl.BlockSpec((1, q_tile, hw), lambda b, h, i: (b, i, n_hblk + h)),
                pl.BlockSpec((1, q_tile, hw), lambda b, h, i: (b, i, h))]
    args = [q, q, z]
    for kv, krp in segments:
        lk = kv.shape[1]
        in_specs += [pl.BlockSpec((1, lk, 2 * hw), lambda b, h, i: (b, 0, h)),
                     pl.BlockSpec((1, lk, LANES), lambda b, h, i: (b, 0, 0))]
        args += [kv, krp]
    return pl.pallas_call(
        functools.partial(_mla_attn_kernel, len(segments), heads, sub_rows),
        grid=(bsz, n_hblk, lq // q_tile),
        in_specs=in_specs,
        out_specs=pl.BlockSpec((1, q_tile, hw), lambda b, h, i: (b, i, h)),
        out_shape=jax.ShapeDtypeStruct((bsz, lq, MLA_WIDTH), BF16),
        compiler_params=_params(3),
        name="mla_attention",
    )(*args)


def _mla_ctx_kernel(qn_ref, qr_ref, z_ref, kv_ref, kr_ref, x_ref, mod_ref, wo32_ref, lng_ref, lnb_ref, y_ref,
                    a_ref, wo_ref):
    _cast_weights_once(1, [(wo32_ref, wo_ref)])
    _mla_attn_kernel(1, MLA_HEADS, ATTN_ROWS, qn_ref, qr_ref, z_ref, kv_ref, kr_ref, a_ref)
    y_ref[0] = _out_epilogue(a_ref, x_ref, mod_ref, wo_ref, lng_ref, lnb_ref)


def _ctx_fused_specs(seq_len, slot):
    const2 = lambda b: (0, 0)
    tok = pl.BlockSpec((1, seq_len, D_MODEL), lambda b: (b, 0, 0))
    return tok, [tok,
                 pl.BlockSpec((1, 1, 3 * D_MODEL), lambda b: (b, 0, 0)),
                 _weight_spec((1, D_MODEL, D_MODEL), (slot, 0, 0)),
                 pl.BlockSpec((1, D_MODEL), const2),
                 pl.BlockSpec((1, D_MODEL), const2)]


def _mla_ctx_layer(q, z, kv, krp, x, mod, slot, w_out, ln_g, ln_b):
    bsz, seq_len, _ = x.shape
    n_nope = MLA_HEADS * QK_NOPE
    tok, tail_specs = _ctx_fused_specs(seq_len, slot)
    return pl.pallas_call(
        _mla_ctx_kernel,
        grid=(bsz,),
        in_specs=[pl.BlockSpec((1, seq_len, n_nope), lambda b: (b, 0, 0)),
                  pl.BlockSpec((1, seq_len, n_nope), lambda b: (b, 0, 1)),
                  tok,
                  pl.BlockSpec((1, seq_len, kv.shape[2]), lambda b: (b, 0, 0)),
                  pl.BlockSpec((1, seq_len, LANES), lambda b: (b, 0, 0))] + tail_specs,
        out_specs=tok,
        out_shape=jax.ShapeDtypeStruct(x.shape, F32),
        scratch_shapes=[pltpu.VMEM((1, seq_len, MLA_WIDTH), BF16), pltpu.VMEM((D_MODEL, D_MODEL), BF16)],
        compiler_params=_params(1),
        name="mla_ctx_layer",
    )(q, q, z, kv, krp, x, mod, w_out, ln_g, ln_b)


NA_Q_SCALE = NA_SCALE * LOG2E


def _na_proj_kernel(emit_state, fuse_prev, *refs):
    if fuse_prev:
        a_ref, x_ref, pmod_ref, pwo32_ref, plng_ref, plnb_ref, mod_ref, w32_ref = refs[:8]
        refs = refs[8:]
    else:
        x_ref, mod_ref, w32_ref = refs[:3]
        refs = refs[3:]
    q_ref, k_ref, v_ref, z_ref = refs[:4]
    refs = refs[4:]
    if emit_state:
        sk_ref, sv_ref = refs[:2]
        refs = refs[2:]
    if fuse_prev:
        y_ref, w_ref, pwo_ref = refs
        _cast_weights_once(2, [(w32_ref, w_ref), (pwo32_ref, pwo_ref)])
        x = _out_epilogue(a_ref, x_ref, pmod_ref, pwo_ref, plng_ref, plnb_ref)
        y_ref[0] = x
    else:
        (w_ref,) = refs
        _cast_weights_once(2, [(w32_ref, w_ref)])
        x = x_ref[0]
    shift, scale, _ = _modulation(mod_ref)
    h = (x * (1.0 + scale) + shift).astype(BF16)
    w = NA_WIDTH
    q_ref[0] = (_dot(h, w_ref[:, :w]) * NA_Q_SCALE).astype(BF16)
    k = _dot(h, w_ref[:, w:2 * w])
    v = _dot(h, w_ref[:, 2 * w:3 * w])
    k_ref[0] = k.astype(BF16)
    v_ref[0] = v.astype(BF16)
    z_ref[0] = _dot(h, w_ref[:, 3 * w:])
    if emit_state:
        sk_ref[...] = k.reshape(sk_ref.shape)
        sv_ref[...] = v.reshape(sv_ref.shape)


def _na_proj(x, mod, slot, w_in, emit_state, prev=None):
    bsz, seq_len, _ = x.shape
    t_rows = min(ROW_TILE, seq_len)
    const2 = lambda b, i: (0, 0)
    tok = pl.BlockSpec((1, t_rows, NA_WIDTH), lambda b, i: (b, i, 0))
    mod_spec = pl.BlockSpec((1, 1, 3 * D_MODEL), lambda b, i: (b, 0, 0))
    w_spec = _weight_spec((1, D_MODEL, 4 * NA_WIDTH), (slot, 0, 0))
    out_specs = [tok, tok, tok, tok]
    act = jax.ShapeDtypeStruct((bsz, seq_len, NA_WIDTH), BF16)
    out_shape = [act, act, act, jax.ShapeDtypeStruct((bsz, seq_len, NA_WIDTH), F32)]
    scratch = [pltpu.VMEM((D_MODEL, 4 * NA_WIDTH), BF16)]
    if emit_state:
        st_spec, st_shape = _state_out(bsz, seq_len, t_rows, emit_state)
        out_specs += [st_spec(NA_WIDTH)] * 2
        out_shape += [st_shape(NA_WIDTH)] * 2
    if prev is None:
        in_specs = [tok, mod_spec, w_spec]
        args = [x, mod, w_in]
    else:
        a, mod_prev, slot_prev, w_out_prev, ln_g_prev, ln_b_prev = prev
        in_specs = [tok, tok, mod_spec, _weight_spec((1, D_MODEL, D_MODEL), (slot_prev, 0, 0)),
                    pl.BlockSpec((1, D_MODEL), const2), pl.BlockSpec((1, D_MODEL), const2), mod_spec, w_spec]
        args = [a, x, mod_prev, w_out_prev, ln_g_prev, ln_b_prev, mod, w_in]
        out_specs.append(tok)
        out_shape.append(jax.ShapeDtypeStruct(x.shape, F32))
        scratch.append(pltpu.VMEM((D_MODEL, D_MODEL), BF16))
    return pl.pallas_call(
        functools.partial(_na_proj_kernel, emit_state, prev is not None),
        grid=(bsz, seq_len // t_rows),
        in_specs=in_specs,
        out_specs=out_specs,
        out_shape=out_shape,
        scratch_shapes=scratch,
        compiler_params=_params(2),
        name="na_proj",
    )(*args)


def _split_head_pair(q2, low_half):
    zero = jnp.zeros_like(q2)
    return jnp.concatenate([jnp.where(low_half, q2, zero), jnp.where(low_half, zero, q2)], axis=0)


def _na_ctx_attn_kernel(q_ref, k_ref, v_ref, z_ref, o_ref):
    n = q_ref.shape[1]
    low_half = lax.broadcasted_iota(jnp.int32, (1, LANES), 1) < NA_HEAD_DIM
    for j in range(NA_PAIRS):
        lanes = slice(j * LANES, (j + 1) * LANES)
        qs = _split_head_pair(q_ref[0, :, lanes], low_half)
        o = _softmax_pv([_dot_nt(qs, k_ref[0, :, lanes])], [v_ref[0, :, lanes]])
        o2 = jnp.where(low_half, o[:n], o[n:])
        o_ref[0, :, lanes] = (o2 * _silu(z_ref[0, :, lanes])).astype(BF16)


def _na_ctx_kernel(q_ref, k_ref, v_ref, z_ref, x_ref, mod_ref, wo32_ref, lng_ref, lnb_ref, y_ref, a_ref, wo_ref):
    _cast_weights_once(1, [(wo32_ref, wo_ref)])
    _na_ctx_attn_kernel(q_ref, k_ref, v_ref, z_ref, a_ref)
    y_ref[0] = _out_epilogue(a_ref, x_ref, mod_ref, wo_ref, lng_ref, lnb_ref)


def _na_ctx_layer(q, k, v, z, x, mod, slot, w_out, ln_g, ln_b):
    bsz, seq_len, _ = x.shape
    tok, tail_specs = _ctx_fused_specs(seq_len, slot)
    return pl.pallas_call(
        _na_ctx_kernel,
        grid=(bsz,),
        in_specs=[tok, tok, tok, tok] + tail_specs,
        out_specs=tok,
        out_shape=jax.ShapeDtypeStruct(x.shape, F32),
        scratch_shapes=[pltpu.VMEM((1, seq_len, NA_WIDTH), BF16), pltpu.VMEM((D_MODEL, D_MODEL), BF16)],
        compiler_params=_params(1),
        name="na_ctx_layer",
    )(q, k, v, z, x, mod, w_out, ln_g, ln_b)


NA_N_DR = 2 * NA_WIN_R - 1
NA_BIAS_SLOTS = NA_N_DR + 1


def _na_bias_kernel(rpb_ref, o_ref):
    qc = lax.broadcasted_iota(jnp.int32, (GRID_W, LANES), 0)
    lane = lax.broadcasted_iota(jnp.int32, (GRID_W, LANES), 1)
    low_half = lane < GRID_W
    kc = jnp.where(low_half, lane, lane - GRID_W)
    cstart = jnp.clip(qc - NA_WIN_C // 2, 0, GRID_W - NA_WIN_C)
    valid = (kc >= cstart) & (kc < cstart + NA_WIN_C)
    for h in range(rpb_ref.shape[0]):
        for i in range(NA_BIAS_SLOTS):
            row_l = jnp.broadcast_to(rpb_ref[h, pl.ds(max(i - 1, 0), 1), :], (GRID_W, LANES))
            row_r = jnp.broadcast_to(rpb_ref[h, pl.ds(min(i, NA_N_DR - 1), 1), :], (GRID_W, LANES))
            left = pltpu.roll(row_l, LANES - (NA_WIN_C - 1), 1, stride=1, stride_axis=0)
            right = pltpu.roll(row_r, GRID_W - (NA_WIN_C - 1), 1, stride=1, stride_axis=0)
            tile = jnp.where(low_half, left, right)
            o_ref[h, i] = jnp.where(valid, tile * LOG2E, NEG_INF)


def _na_bias_table(rpb):
    n_dr, n_dc = rpb.shape[1:]
    rpb_rows = jnp.pad(rpb, ((0, 0), (0, 16 - n_dr), (0, LANES - n_dc)))
    heads_per_step = 4
    return pl.pallas_call(
        _na_bias_kernel,
        grid=(NA_HEADS // heads_per_step,),
        in_specs=[pl.BlockSpec((heads_per_step, 16, LANES), lambda h: (h, 0, 0))],
        out_specs=pl.BlockSpec((heads_per_step, NA_BIAS_SLOTS, GRID_W, LANES), lambda h: (h, 0, 0, 0)),
        out_shape=jax.ShapeDtypeStruct((NA_HEADS, NA_BIAS_SLOTS, GRID_W, LANES), F32),
        compiler_params=_params(1),
        name="na_bias_table",
    )(rpb_rows)


NA_ROWS_PER_GROUP = 4
NA_BAND_ROWS = NA_ROWS_PER_GROUP + NA_WIN_R
NA_GROUPS_PER_STEP = 8


def _na_lat_attn_kernel(n_rows, n_groups, q_ref, k_ref, v_ref, kc_ref, vc_ref, b_ref, z_ref, o_ref):
    n_q = NA_ROWS_PER_GROUP * GRID_W
    low_half = lax.broadcasted_iota(jnp.int32, (1, LANES), 1) < NA_HEAD_DIM
    for grp in range(n_groups):
        rows = slice(grp * n_q, (grp + 1) * n_q)
        r0 = (pl.program_id(2) * n_groups + grp) * NA_ROWS_PER_GROUP
        band0 = jnp.clip(r0 - NA_WIN_R // 2, 0, n_rows - NA_BAND_ROWS)
        qs = _split_head_pair(q_ref[0, rows, :], low_half)
        key0 = pl.multiple_of(band0 * GRID_W, GRID_W)
        kl = k_ref[0, pl.ds(key0, NA_BAND_ROWS * GRID_W), :]
        vl = v_ref[0, pl.ds(key0, NA_BAND_ROWS * GRID_W), :]
        bias_rows = []
        for rr in range(NA_ROWS_PER_GROUP):
            r = r0 + rr
            win0 = jnp.clip(r - NA_WIN_R // 2, 0, n_rows - NA_WIN_R)
            blocks = []
            for cblk in range(NA_BAND_ROWS // 2):
                key_row = band0 + 2 * cblk
                slot = jnp.clip(key_row - r + NA_WIN_R, 0, NA_BIAS_SLOTS - 1)
                left_pen = jnp.where((key_row >= win0) & (key_row < win0 + NA_WIN_R), 0.0, NEG_INF)
                right_pen = jnp.where((key_row + 1 >= win0) & (key_row + 1 < win0 + NA_WIN_R), 0.0, NEG_INF)
                blocks.append((slot, jnp.where(low_half, left_pen, right_pen)))
            bias_rows.append(blocks)
        bias = jnp.concatenate(
            [jnp.concatenate(
                [jnp.concatenate([b_ref[e, slot] + pen for slot, pen in blocks], axis=1)
                 for blocks in bias_rows], axis=0)
             for e in range(2)], axis=0)
        o = _softmax_pv([_dot_nt(qs, kl) + bias, _dot_nt(qs, kc_ref[0])], [vl, vc_ref[0]])
        o2 = jnp.where(low_half, o[:n_q], o[n_q:])
        o_ref[0, rows, :] = (o2 * _silu(z_ref[0, rows, :])).astype(BF16)


def _na_lat_attention(q, k, v, k_ctx, v_ctx, bias, z):
    bsz, seq_len, _ = q.shape
    n_rows = seq_len // GRID_W
    past = k_ctx.shape[1]
    n_groups = min(NA_GROUPS_PER_STEP, n_rows // NA_ROWS_PER_GROUP)
    rows_per_step = n_groups * NA_ROWS_PER_GROUP
    row_blk = pl.BlockSpec((1, rows_per_step * GRID_W, LANES), lambda b, j, r: (b, r, j))
    seq_blk = pl.BlockSpec((1, seq_len, LANES), lambda b, j, r: (b, 0, j))
    ctx_blk = pl.BlockSpec((1, past, LANES), lambda b, j, r: (b, 0, j))
    return pl.pallas_call(
        functools.partial(_na_lat_attn_kernel, n_rows, n_groups),
        grid=(bsz, NA_PAIRS, n_rows // rows_per_step),
        in_specs=[row_blk, seq_blk, seq_blk, ctx_blk, ctx_blk,
                  pl.BlockSpec((2, NA_BIAS_SLOTS, GRID_W, LANES), lambda b, j, r: (j, 0, 0, 0)),
                  row_blk],
        out_specs=row_blk,
        out_shape=jax.ShapeDtypeStruct((bsz, seq_len, NA_WIDTH), BF16),
        compiler_params=_params(3),
        name="na_lat_attention",
    )(q, k, v, k_ctx, v_ctx, bias, z)


def _rope_swap_perm():
    nf = QK_ROPE // 4
    idx = np.arange(QK_ROPE).reshape(2, 2, nf)
    return idx[:, ::-1, :].reshape(-1)


def _rope_table(seq_len):
    nf = QK_ROPE // 4
    inv = ROPE_BASE ** (-jnp.arange(nf, dtype=F32) / nf)
    t = jnp.arange(seq_len)
    pos = jnp.stack([t // GRID_W, t % GRID_W], -1).astype(F32)
    ang = pos[:, :, None] * inv
    cos = jnp.cos(ang)
    sin = jnp.sin(ang)
    c = jnp.concatenate([cos, cos], axis=-1).reshape(seq_len, QK_ROPE)
    s = jnp.concatenate([-sin, sin], axis=-1).reshape(seq_len, QK_ROPE)
    return jnp.concatenate([c, s], axis=-1)


def _mla_weights(w_in, w_uq):
    perm = _rope_swap_perm()
    o1, o2, o3 = Q_LORA, Q_LORA + KV_LORA, Q_LORA + KV_LORA + QK_ROPE
    w_kr = w_in[:, o2:o3]
    w1 = jnp.concatenate([w_in[:, :o2], w_kr, w_kr[:, perm], w_in[:, o3:]], axis=1).astype(BF16)
    wq = w_uq.reshape(Q_LORA, MLA_HEADS, QK_NOPE + QK_ROPE)
    nope = wq[:, :, :QK_NOPE].reshape(Q_LORA, MLA_HEADS * QK_NOPE)
    rope = wq[:, :, QK_NOPE:]
    rope_blk = jnp.concatenate([rope, rope[:, :, perm]], axis=-1).reshape(Q_LORA, MLA_HEADS * LANES)
    return w1, jnp.concatenate([nope, rope_blk], axis=1).astype(BF16)


def kernel(x_prompt, x_sample, cache_mla_ckv, cache_mla_krope, cache_na_k, cache_na_v, c, c_ctx, ada_w, ada_b, ln_g, ln_b, pool_w_in, pool_w_grp, pool_scale, pool_w_out, mla_w_in, mla_q_norm, mla_w_uq, mla_kv_norm, mla_w_ukv, mla_w_out, na_w_in, na_rpb, na_w_out):
    n_ctx, n_lat = x_prompt.shape[0], x_sample.shape[0]
    lat_len = x_sample.shape[1]
    cond = jnp.concatenate([c_ctx[None], c, jnp.zeros((8 - 1 - n_lat, D_MODEL), F32)], axis=0)
    mod = _ada(cond, ada_w, ada_b)
    yp, ys = x_prompt, x_sample
    st_ckv = st_kr = st_k = st_v = None
    ctx_len = x_prompt.shape[1]
    ctx_fold = 2 if (n_ctx % 2 == 0 and 2 * ctx_len <= ROW_TILE) else 1
    fold = lambda t: t.reshape((n_ctx // ctx_fold, ctx_fold * ctx_len) + t.shape[2:])
    unfold = lambda t: t.reshape((n_ctx, ctx_len) + t.shape[2:])
    pending = None
    for i in range(DEPTH):
        kind, j = i % N_MIXERS, i // N_MIXERS
        mod_p = jnp.broadcast_to(mod[i, 0:1][None], (n_ctx, 1, 3 * D_MODEL))
        mod_pf = mod_p[:n_ctx // ctx_fold]
        mod_s = mod[i, 1:1 + n_lat][:, None, :]
        g, b = ln_g[i][None], ln_b[i][None]
        if kind != 2 and pending is not None:
            ys = _out_layer(pending[0], ys, *pending[1:])
            pending = None
        if kind == 0:
            ps = pool_scale[j][None]
            yp = unfold(_pool_layer(fold(yp), ctx_len, mod_pf, j, pool_w_in, pool_w_grp, ps, pool_w_out, g, b))
            ys = _pool_layer(ys, lat_len, mod_s, j, pool_w_in, pool_w_grp, ps, pool_w_out, g, b)
        elif kind == 1:
            w1, w_uq = _mla_weights(mla_w_in[j], mla_w_uq[j])
            w_ukv = mla_w_ukv[j].astype(BF16)
            qn, kvn = mla_q_norm[j][None], mla_kv_norm[j][None]
            q_p, kv_p, krp_p, z_p, st_ckv, st_kr = _mla_proj(fold(yp), mod_pf, w1, qn, kvn, w_uq, w_ukv, None, ctx_fold)
            yp = _mla_ctx_layer(unfold(q_p), unfold(z_p), unfold(kv_p), unfold(krp_p), yp, mod_p, j, mla_w_out, g, b)
            q_s, kv_s, krp_s, z_s = _mla_proj(ys, mod_s, w1, qn, kvn, w_uq, w_ukv, _rope_table(lat_len), False)
            kv_c, krp_c = _mla_cache_keys(cache_mla_ckv, cache_mla_krope, j, w_ukv)
            a_s = _mla_attention(q_s, z_s, [(kv_c, krp_c), (kv_s, krp_s)], lat_len, 2, 2 * ATTN_ROWS)
            pending = (a_s, mod_s, j, mla_w_out, g, b)
        else:
            q_p, k_p, v_p, z_p, st_k, st_v = _na_proj(fold(yp), mod_pf, j, na_w_in, ctx_fold)
            yp = _na_ctx_layer(unfold(q_p), unfold(k_p), unfold(v_p), unfold(z_p), yp, mod_p, j, na_w_out, g, b)
            if pending is None:
                q_s, k_s, v_s, z_s = _na_proj(ys, mod_s, j, na_w_in, False)
            else:
                q_s, k_s, v_s, z_s, ys = _na_proj(ys, mod_s, j, na_w_in, False, prev=pending)
                pending = None
            past = cache_na_k.shape[2]
            k_c = cache_na_k[:, j].reshape(n_lat, past, NA_WIDTH).astype(BF16)
            v_c = cache_na_v[:, j].reshape(n_lat, past, NA_WIDTH).astype(BF16)
            bias = _na_bias_table(na_rpb[j])
            a_s = _na_lat_attention(q_s, k_s, v_s, k_c, v_c, bias, z_s)
            pending = (a_s, mod_s, j, na_w_out, g, b)
    if pending is not None:
        ys = _out_layer(pending[0], ys, *pending[1:])
    state_na_k = st_k.reshape(n_ctx, 1, -1, NA_HEADS, NA_HEAD_DIM)
    state_na_v = st_v.reshape(n_ctx, 1, -1, NA_HEADS, NA_HEAD_DIM)
    return (yp, ys, st_ckv, st_kr, state_na_k, state_na_v)
```

```python
import functools
import math

import numpy as np
import jax
import jax.numpy as jnp
from jax import lax
from jax.experimental import pallas as pl
from jax.experimental.pallas import tpu as pltpu

F32 = jnp.float32
BF16 = jnp.bfloat16

D_MODEL = 1024
DEPTH = 4
GRID_W = 64
N_MIXERS = 3

POOL_WINDOWS = (2, 4, 8, 16)
POOL_GROUP_DIM = D_MODEL // len(POOL_WINDOWS)
POOL_HALO = 8

MLA_HEADS = 8
QK_NOPE = 128
QK_ROPE = 64
V_HEAD = 128
Q_LORA = D_MODEL // 2
KV_LORA = D_MODEL // 4
MLA_WIDTH = MLA_HEADS * V_HEAD
MLA_SCALE = (QK_NOPE + QK_ROPE) ** -0.5

NA_HEADS = 16
NA_HEAD_DIM = 64
NA_WIDTH = NA_HEADS * NA_HEAD_DIM
NA_WIN_R = 8
NA_WIN_C = 16
NA_SCALE = NA_HEAD_DIM ** -0.5
NA_PAIRS = NA_HEADS // 2

ROPE_BASE = 10000.0
LN_EPS = 1e-5
RMS_EPS = 1e-6
NEG_INF = -1e30
DEEPNORM_ALPHA = (2 * DEPTH) ** 0.25
LOG2E = math.log2(math.e)

LANES = 128
ROW_TILE = 512
ATTN_ROWS = 256
VMEM_LIMIT = 56 * 1024 * 1024


def _params(n_axes):
    return pltpu.CompilerParams(dimension_semantics=("arbitrary",) * n_axes, vmem_limit_bytes=VMEM_LIMIT)


def _silu(x):
    h = 0.5 * x
    return h + h * jnp.tanh(h)


def _dot(a, b):
    return jnp.dot(a, b, preferred_element_type=F32)


def _dot_nt(a, b):
    return lax.dot_general(a, b, (((1,), (1,)), ((), ())), preferred_element_type=F32)


def _weight_spec(block_shape, index):
    return pl.BlockSpec(block_shape, lambda *_: index, pipeline_mode=pl.Buffered(1))


def _mod_spec(mod):
    _, first_row, per_batch = mod
    return pl.BlockSpec((1, 1, 3 * D_MODEL), lambda b, *_: (first_row + b if per_batch else first_row, 0, 0))


def _row_spec(param):
    rows, row = param
    return pl.BlockSpec((None, 1, rows.shape[2]), lambda *_: (row, 0, 0))


def _cast_weights_once(n_axes, pairs):
    first = functools.reduce(jnp.logical_and, [pl.program_id(a) == 0 for a in range(n_axes)])

    @pl.when(first)
    def _():
        for src, dst in pairs:
            dst[...] = src[...].reshape(dst.shape).astype(BF16)


def _state_out(bsz, seq_len, t_rows, n_seq):
    assert t_rows == seq_len and seq_len % n_seq == 0
    spec = lambda width: pl.BlockSpec((n_seq, 1, seq_len // n_seq, width), lambda b, i: (b, 0, 0, 0))
    shape = lambda width: jax.ShapeDtypeStruct((bsz * n_seq, 1, seq_len // n_seq, width), F32)
    return spec, shape


def _modulation(mod_ref):
    m = mod_ref[0]
    return m[:, :D_MODEL], m[:, D_MODEL:2 * D_MODEL], m[:, 2 * D_MODEL:]


def _deepnorm_ln(x, gate, branch, g, b):
    xf = DEEPNORM_ALPHA * x + gate * branch
    mu = jnp.mean(xf, axis=-1, keepdims=True)
    xc = xf - mu
    var = jnp.mean(xc * xc, axis=-1, keepdims=True)
    return xc * lax.rsqrt(var + LN_EPS) * g + b


def _rms(x, g):
    return x * lax.rsqrt(jnp.mean(x * x, axis=-1, keepdims=True) + RMS_EPS) * g


def _ada_kernel(cond_ref, w_ref, b_ref, o_ref):
    c = cond_ref[...]
    o_ref[0] = _dot(_silu(c).astype(BF16), w_ref[0].astype(BF16)) + b_ref[0]


def _ada(cond, ada_w, ada_b):
    tn = D_MODEL
    return pl.pallas_call(
        _ada_kernel,
        grid=(DEPTH, 3 * D_MODEL // tn),
        in_specs=[
            pl.BlockSpec((8, D_MODEL), lambda l, n: (0, 0)),
            pl.BlockSpec((1, D_MODEL, tn), lambda l, n: (l, 0, n)),
            pl.BlockSpec((1, 1, tn), lambda l, n: (l, 0, n)),
        ],
        out_specs=pl.BlockSpec((1, 8, tn), lambda l, n: (l, 0, n)),
        out_shape=jax.ShapeDtypeStruct((DEPTH, 8, 3 * D_MODEL), F32),
        compiler_params=_params(2),
        name="ada",
    )(cond, ada_w, ada_b.reshape(DEPTH, 1, 3 * D_MODEL))


def _rows_ahead(x, k):
    return x if k == 0 else pltpu.roll(x, x.shape[0] - k, 0)


def _pool_chain(seq_len, t0, x, x_prev, x_next, mods, weights, ln):
    c_rows = x.shape[0]
    shift, scale, gate = mods
    wu_ref, wz_ref, wg_ref, ps_ref, wo_ref = weights
    one_scale = 1.0 + scale
    hm = x * one_scale + shift
    hp = x_prev * one_scale + shift
    hn = x_next * one_scale + shift
    hext = jnp.concatenate([hm, hp, hn], axis=0).astype(BF16)
    u = _dot(hext, wu_ref[...])
    z = _dot(hext[:c_rows], wz_ref[...])
    prev_ok = (t0 > 0).astype(F32)
    next_ok = (t0 + c_rows < seq_len).astype(F32)
    centre = u[:c_rows]
    useq = jnp.concatenate([u[c_rows:c_rows + POOL_HALO] * prev_ok, centre, u[c_rows + POOL_HALO:] * next_ok], axis=0)

    t = t0 + lax.broadcasted_iota(jnp.int32, (c_rows, 1), 0)
    mixed = []
    for g, w in enumerate(POOL_WINDOWS):
        cols = slice(g * POOL_GROUP_DIM, (g + 1) * POOL_GROUP_DIM)
        q = useq[:, cols]
        span = 1
        while span < w // 2:
            q = q + _rows_ahead(q, span)
            span *= 2
        acc = _rows_ahead(q, POOL_HALO - w // 2)[:c_rows] + q[POOL_HALO:POOL_HALO + c_rows]
        lo = jnp.clip(t - w // 2, 0, seq_len)
        hi = jnp.clip(t + (w - w // 2), 0, seq_len)
        inv_cnt = 1.0 / (hi - lo).astype(F32)
        mg = (acc * inv_cnt - centre[:, cols]).astype(BF16)
        mixed.append(_dot(mg, wg_ref[g]))
    mixed = jnp.concatenate(mixed, axis=1) * ps_ref[...]
    a = (mixed * _silu(z)).astype(BF16)
    branch = _dot(a, wo_ref[...])
    return _deepnorm_ln(x, gate, branch, *ln)


POOL_A_HALO = 16


def _pool_kernel(seq_len, fuse_prev, *refs):
    if fuse_prev:
        a_ref, ap_ref, an_ref, pmod_ref, pwo32_ref, plng_ref, plnb_ref = refs[:7]
        refs = refs[7:]
    x_ref, xp_ref, xn_ref, mod_ref, wu32_ref, wz32_ref, wg32_ref, ps_ref, wo32_ref, lng_ref, lnb_ref, o_ref = refs[:12]
    wu_ref, wz_ref, wg_ref, wo_ref = refs[12:16]
    casts = [(wu32_ref, wu_ref), (wz32_ref, wz_ref), (wg32_ref, wg_ref), (wo32_ref, wo_ref)]
    if fuse_prev:
        pwo_ref = refs[16]
        casts.append((pwo32_ref, pwo_ref))
    _cast_weights_once(2, casts)
    t_rows = x_ref.shape[1]
    c_rows = min(t_rows, seq_len)
    n_chain = t_rows // c_rows
    mods = _modulation(mod_ref)
    weights = (wu_ref, wz_ref, wg_ref, ps_ref, wo_ref)
    ln = (lng_ref[...], lnb_ref[...])
    x_all, x_before, x_after = x_ref[0], xp_ref[0], xn_ref[0]
    if fuse_prev:
        _, _, pgate = _modulation(pmod_ref)
        pln = (plng_ref[...], plnb_ref[...])
        x_all = _deepnorm_ln(x_all, pgate, _dot(a_ref[0], pwo_ref[...]), *pln)
        x_before = _deepnorm_ln(x_before, pgate, _dot(ap_ref[0], pwo_ref[...])[POOL_A_HALO - POOL_HALO:], *pln)
        x_after = _deepnorm_ln(x_after, pgate, _dot(an_ref[0], pwo_ref[...])[:POOL_HALO], *pln)
    for ci in range(n_chain):
        r0 = ci * c_rows
        t0 = lax.rem(pl.program_id(1) * t_rows + r0, seq_len)
        x_prev = x_before if ci == 0 else x_all[r0 - POOL_HALO:r0]
        x_next = x_after if ci == n_chain - 1 else x_all[r0 + c_rows:r0 + c_rows + POOL_HALO]
        o_ref[0, r0:r0 + c_rows, :] = _pool_chain(seq_len, t0, x_all[r0:r0 + c_rows], x_prev, x_next,
                                                  mods, weights, ln)


def _pool_layer(x, seq_len, mod, slot, w_in, w_grp, p_scale, w_out, ln_g, ln_b, prev=None):
    bsz, row_len, _ = x.shape
    n_grp = len(POOL_WINDOWS)
    t_rows = min(ROW_TILE, row_len)
    assert row_len % seq_len == 0 and (seq_len % t_rows == 0 or t_rows % seq_len == 0)
    n_tiles = row_len // t_rows
    xh = x.reshape(bsz * row_len // POOL_HALO, POOL_HALO, D_MODEL)

    def halo_specs(rows):
        per_tile, per_seq = t_rows // rows, row_len // rows
        last = bsz * per_seq - 1
        return [pl.BlockSpec((1, rows, D_MODEL), lambda b, i: (jnp.maximum(b * per_seq + i * per_tile - 1, 0), 0, 0)),
                pl.BlockSpec((1, rows, D_MODEL), lambda b, i: (jnp.minimum(b * per_seq + (i + 1) * per_tile, last), 0, 0))]

    tok = pl.BlockSpec((1, t_rows, D_MODEL), lambda b, i: (b, i, 0))
    in_specs = [tok] + halo_specs(POOL_HALO) + [
        _mod_spec(mod),
        _weight_spec((1, D_MODEL, D_MODEL), (slot, 0, 0)),
        _weight_spec((1, D_MODEL, D_MODEL), (slot, 0, 1)),
        _weight_spec((1, n_grp, POOL_GROUP_DIM, POOL_GROUP_DIM), (slot, 0, 0, 0)),
        _row_spec(p_scale),
        _weight_spec((1, D_MODEL, D_MODEL), (slot, 0, 0)),
        _row_spec(ln_g),
        _row_spec(ln_b)]
    args = [x, xh, xh, mod[0], w_in, w_in, w_grp, p_scale[0], w_out, ln_g[0], ln_b[0]]
    scratch = [pltpu.VMEM((D_MODEL, D_MODEL), BF16),
               pltpu.VMEM((D_MODEL, D_MODEL), BF16),
               pltpu.VMEM((n_grp, POOL_GROUP_DIM, POOL_GROUP_DIM), BF16),
               pltpu.VMEM((D_MODEL, D_MODEL), BF16)]
    if prev is not None:
        a, mod_prev, slot_prev, w_out_prev, ln_g_prev, ln_b_prev = prev
        ah = a.reshape(bsz * row_len // POOL_A_HALO, POOL_A_HALO, D_MODEL)
        in_specs = [tok] + halo_specs(POOL_A_HALO) + [
            _mod_spec(mod_prev), _weight_spec((1, D_MODEL, D_MODEL), (slot_prev, 0, 0)),
            _row_spec(ln_g_prev), _row_spec(ln_b_prev)] + in_specs
        args = [a, ah, ah, mod_prev[0], w_out_prev, ln_g_prev[0], ln_b_prev[0]] + args
        scratch.append(pltpu.VMEM((D_MODEL, D_MODEL), BF16))
    return pl.pallas_call(
        functools.partial(_pool_kernel, seq_len, prev is not None),
        grid=(bsz, n_tiles),
        in_specs=in_specs,
        out_specs=tok,
        out_shape=jax.ShapeDtypeStruct(x.shape, F32),
        scratch_shapes=scratch,
        compiler_params=_params(2),
        name="pool_layer",
    )(*args)


def _out_epilogue(a_ref, x_ref, mod_ref, wo_ref, lng_ref, lnb_ref):
    _, _, gate = _modulation(mod_ref)
    branch = _dot(a_ref[0], wo_ref[...])
    return _deepnorm_ln(x_ref[0], gate, branch, lng_ref[...], lnb_ref[...])


def _out_kernel(a_ref, x_ref, mod_ref, wo32_ref, lng_ref, lnb_ref, o_ref, wo_ref):
    _cast_weights_once(2, [(wo32_ref, wo_ref)])
    o_ref[0] = _out_epilogue(a_ref, x_ref, mod_ref, wo_ref, lng_ref, lnb_ref)


def _out_layer(a, x, mod, slot, w_out, ln_g, ln_b):
    bsz, seq_len, _ = x.shape
    t_rows = min(ROW_TILE, seq_len)
    const2 = lambda b, i: (0, 0)
    tok = pl.BlockSpec((1, t_rows, D_MODEL), lambda b, i: (b, i, 0))
    return pl.pallas_call(
        _out_kernel,
        grid=(bsz, seq_len // t_rows),
        in_specs=[tok, tok, _mod_spec(mod), _weight_spec((1, D_MODEL, D_MODEL), (slot, 0, 0)),
                  _row_spec(ln_g), _row_spec(ln_b)],
        out_specs=tok,
        out_shape=jax.ShapeDtypeStruct(x.shape, F32),
        scratch_shapes=[pltpu.VMEM((D_MODEL, D_MODEL), BF16)],
        compiler_params=_params(2),
        name="out_layer",
    )(a, x, mod[0], w_out, ln_g[0], ln_b[0])


MLA_W1_COLS = Q_LORA + KV_LORA + 2 * QK_ROPE + MLA_WIDTH
MLA_Z_OFF = Q_LORA + KV_LORA + 2 * QK_ROPE
MLA_Q_COLS = MLA_HEADS * QK_NOPE + MLA_HEADS * LANES
MLA_Q_SCALE = MLA_SCALE * LOG2E


def _rope_block(blk, cs, low_half):
    if cs is not None:
        blk = blk * cs
        blk = blk + pltpu.roll(blk, LANES // 2, 1)
    return jnp.where(low_half, blk, 0.0)


def _mla_proj_kernel(rope, emit_state, *refs):
    x_ref, mod_ref, w1_ref, qn_ref, kvn_ref, wuq_ref, wukv_ref = refs[:7]
    refs = refs[7:]
    cs = None
    if rope:
        cs = refs[0][...]
        refs = refs[1:]
    q_ref, kv_ref, krp_ref, z_ref = refs[:4]
    shift, scale, _ = _modulation(mod_ref)
    h = (x_ref[0] * (1.0 + scale) + shift).astype(BF16)
    p = _dot(h, w1_ref[...])
    cq = p[:, :Q_LORA]
    ckv = p[:, Q_LORA:Q_LORA + KV_LORA]
    krb = p[:, Q_LORA + KV_LORA:MLA_Z_OFF]
    z_ref[0] = p[:, MLA_Z_OFF:]
    ckvn = _rms(ckv, kvn_ref[...])
    if emit_state:
        ckv_ref, kr_ref = refs[4:6]
        ckv_ref[...] = ckvn.reshape(ckv_ref.shape)
        kr_ref[...] = krb[:, :QK_ROPE].reshape(kr_ref.shape)
    low_half = lax.broadcasted_iota(jnp.int32, (1, LANES), 1) < LANES // 2
    krp_ref[0] = _rope_block(krb, cs, low_half).astype(BF16)
    kv_ref[0] = _dot(ckvn.astype(BF16), wukv_ref[...]).astype(BF16)
    qf = _dot(_rms(cq, qn_ref[...]).astype(BF16), wuq_ref[...])
    n_nope = MLA_HEADS * QK_NOPE
    q_ref[0, :, :n_nope] = (qf[:, :n_nope] * MLA_Q_SCALE).astype(BF16)
    for hh in range(MLA_HEADS):
        cols = slice(n_nope + hh * LANES, n_nope + (hh + 1) * LANES)
        q_ref[0, :, cols] = (_rope_block(qf[:, cols], cs, low_half) * MLA_Q_SCALE).astype(BF16)


def _mla_proj(x, mod, w1, q_norm, kv_norm, w_uq, w_ukv, rope_cs, emit_state):
    bsz, seq_len, _ = x.shape
    t_rows = min(ROW_TILE, seq_len)
    const2 = lambda b, i: (0, 0)
    tok = lambda width: pl.BlockSpec((1, t_rows, width), lambda b, i: (b, i, 0))
    in_specs = [tok(D_MODEL),
                _mod_spec(mod),
                pl.BlockSpec((D_MODEL, MLA_W1_COLS), const2),
                pl.BlockSpec((1, Q_LORA), const2),
                pl.BlockSpec((1, KV_LORA), const2),
                pl.BlockSpec((Q_LORA, MLA_Q_COLS), const2),
                pl.BlockSpec((KV_LORA, MLA_HEADS * (QK_NOPE + V_HEAD)), const2)]
    args = [x, mod[0], w1, q_norm, kv_norm, w_uq, w_ukv]
    if rope_cs is not None:
        in_specs.append(pl.BlockSpec((t_rows, LANES), lambda b, i: (i, 0)))
        args.append(rope_cs)
    out_specs = [tok(MLA_Q_COLS), tok(MLA_HEADS * (QK_NOPE + V_HEAD)), tok(LANES), tok(MLA_WIDTH)]
    out_shape = [jax.ShapeDtypeStruct((bsz, seq_len, MLA_Q_COLS), BF16),
                 jax.ShapeDtypeStruct((bsz, seq_len, MLA_HEADS * (QK_NOPE + V_HEAD)), BF16),
                 jax.ShapeDtypeStruct((bsz, seq_len, LANES), BF16),
                 jax.ShapeDtypeStruct((bsz, seq_len, MLA_WIDTH), F32)]
    if emit_state:
        st_spec, st_shape = _state_out(bsz, seq_len, t_rows, emit_state)
        out_specs += [st_spec(KV_LORA), st_spec(QK_ROPE)]
        out_shape += [st_shape(KV_LORA), st_shape(QK_ROPE)]
    return pl.pallas_call(
        functools.partial(_mla_proj_kernel, rope_cs is not None, emit_state),
        grid=(bsz, seq_len // t_rows),
        in_specs=in_specs,
        out_specs=out_specs,
        out_shape=out_shape,
        compiler_params=_params(2),
        name="mla_proj",
    )(*args)


def _mla_cache_kernel(ckv_ref, kr_ref, wukv_ref, kv_ref, krp_ref):
    kv_ref[0] = _dot(ckv_ref[0, 0].astype(BF16), wukv_ref[...]).astype(BF16)
    krp_ref[0, :, :QK_ROPE] = kr_ref[0, 0].astype(BF16)
    krp_ref[0, :, QK_ROPE:] = jnp.zeros((kr_ref.shape[2], LANES - QK_ROPE), BF16)


def _mla_cache_keys(cache_ckv, cache_kr, slot, w_ukv):
    bsz, _, past, _ = cache_ckv.shape
    kv_cols = MLA_HEADS * (QK_NOPE + V_HEAD)
    return pl.pallas_call(
        _mla_cache_kernel,
        grid=(bsz,),
        in_specs=[pl.BlockSpec((1, 1, past, KV_LORA), lambda b: (b, slot, 0, 0)),
                  pl.BlockSpec((1, 1, past, QK_ROPE), lambda b: (b, slot, 0, 0)),
                  pl.BlockSpec((KV_LORA, kv_cols), lambda b: (0, 0))],
        out_specs=[pl.BlockSpec((1, past, kv_cols), lambda b: (b, 0, 0)),
                   pl.BlockSpec((1, past, LANES), lambda b: (b, 0, 0))],
        out_shape=[jax.ShapeDtypeStruct((bsz, past, kv_cols), BF16),
                   jax.ShapeDtypeStruct((bsz, past, LANES), BF16)],
        compiler_params=_params(1),
        name="mla_cache_keys",
    )(cache_ckv, cache_kr, w_ukv)


def _softmax_pv(scores, values):
    m = functools.reduce(jnp.maximum, [jnp.max(s, axis=1, keepdims=True) for s in scores])
    denom = 0.0
    acc = 0.0
    for s, v in zip(scores, values):
        p = jnp.exp2(s - m)
        denom = denom + jnp.sum(p, axis=1, keepdims=True)
        acc = acc + _dot(p.astype(BF16), v)
    return acc / denom


def _mla_attn_kernel(n_seg, heads, sub_rows, qn_ref, qr_ref, z_ref, *refs):
    o_ref = refs[2 * n_seg]
    q_rows = qn_ref.shape[1]
    kv_w = QK_NOPE + V_HEAD
    for h in range(heads):
        lanes = slice(h * LANES, (h + 1) * LANES)
        keys, values = [], []
        for s in range(n_seg):
            kv_ref, kr_ref = refs[2 * s:2 * s + 2]
            keys.append(jnp.concatenate([kv_ref[0, :, h * kv_w:h * kv_w + QK_NOPE], kr_ref[0]], axis=1))
            values.append(kv_ref[0, :, h * kv_w + QK_NOPE:(h + 1) * kv_w])
        for r0 in range(0, q_rows, sub_rows):
            rows = slice(r0, r0 + sub_rows)
            q = jnp.concatenate([qn_ref[0, rows, lanes], qr_ref[0, rows, lanes]], axis=1)
            o = _softmax_pv([_dot_nt(q, k) for k in keys], values)
            o_ref[0, rows, lanes] = (o * _silu(z_ref[0, rows, lanes])).astype(BF16)


def _mla_attention(q, z, segments, q_tile, heads, sub_rows):
    bsz, lq, _ = q.shape
    n_hblk = MLA_HEADS // heads
    hw = heads * LANES
    in_specs = [pl.BlockSpec((1, q_tile, hw), lambda b, h, i: (b, i, h)),
                pl.BlockSpec((1, q_tile, hw), lambda b, h, i: (b, i, n_hblk + h)),
                pl.BlockSpec((1, q_tile, hw), lambda b, h, i: (b, i, h))]
    args = [q, q, z]
    for kv, krp in segments:
        lk = kv.shape[1]
        in_specs += [pl.BlockSpec((1, lk, 2 * hw), lambda b, h, i: (b, 0, h)),
                     pl.BlockSpec((1, lk, LANES), lambda b, h, i: (b, 0, 0))]
        args += [kv, krp]
    return pl.pallas_call(
        functools.partial(_mla_attn_kernel, len(segments), heads, sub_rows),
        grid=(bsz, n_hblk, lq // q_tile),
        in_specs=in_specs,
        out_specs=pl.BlockSpec((1, q_tile, hw), lambda b, h, i: (b, i, h)),
        out_shape=jax.ShapeDtypeStruct((bsz, lq, MLA_WIDTH), BF16),
        compiler_params=_params(3),
        name="mla_attention",
    )(*args)


def _mla_ctx_kernel(qn_ref, qr_ref, z_ref, kv_ref, kr_ref, x_ref, mod_ref, wo32_ref, lng_ref, lnb_ref, y_ref,
                    a_ref, wo_ref):
    _cast_weights_once(1, [(wo32_ref, wo_ref)])
    _mla_attn_kernel(1, MLA_HEADS, ATTN_ROWS, qn_ref, qr_ref, z_ref, kv_ref, kr_ref, a_ref)
    y_ref[0] = _out_epilogue(a_ref, x_ref, mod_ref, wo_ref, lng_ref, lnb_ref)


def _ctx_fused_specs(seq_len, slot, mod, ln_g, ln_b):
    tok = pl.BlockSpec((1, seq_len, D_MODEL), lambda b: (b, 0, 0))
    return tok, [tok, _mod_spec(mod), _weight_spec((1, D_MODEL, D_MODEL), (slot, 0, 0)),
                 _row_spec(ln_g), _row_spec(ln_b)]


def _mla_ctx_layer(q, z, kv, krp, x, mod, slot, w_out, ln_g, ln_b):
    bsz, seq_len, _ = x.shape
    n_nope = MLA_HEADS * QK_NOPE
    tok, tail_specs = _ctx_fused_specs(seq_len, slot, mod, ln_g, ln_b)
    return pl.pallas_call(
        _mla_ctx_kernel,
        grid=(bsz,),
        in_specs=[pl.BlockSpec((1, seq_len, n_nope), lambda b: (b, 0, 0)),
                  pl.BlockSpec((1, seq_len, n_nope), lambda b: (b, 0, 1)),
                  tok,
                  pl.BlockSpec((1, seq_len, kv.shape[2]), lambda b: (b, 0, 0)),
                  pl.BlockSpec((1, seq_len, LANES), lambda b: (b, 0, 0))] + tail_specs,
        out_specs=tok,
        out_shape=jax.ShapeDtypeStruct(x.shape, F32),
        scratch_shapes=[pltpu.VMEM((1, seq_len, MLA_WIDTH), BF16), pltpu.VMEM((D_MODEL, D_MODEL), BF16)],
        compiler_params=_params(1),
        name="mla_ctx_layer",
    )(q, q, z, kv, krp, x, mod[0], w_out, ln_g[0], ln_b[0])


NA_Q_SCALE = NA_SCALE * LOG2E


def _na_proj_kernel(emit_state, fuse_prev, *refs):
    if fuse_prev:
        a_ref, x_ref, pmod_ref, pwo32_ref, plng_ref, plnb_ref, mod_ref, w32_ref = refs[:8]
        refs = refs[8:]
    else:
        x_ref, mod_ref, w32_ref = refs[:3]
        refs = refs[3:]
    q_ref, k_ref, v_ref, z_ref = refs[:4]
    refs = refs[4:]
    if emit_state:
        sk_ref, sv_ref = refs[:2]
        refs = refs[2:]
    if fuse_prev:
        y_ref, w_ref, pwo_ref = refs
        _cast_weights_once(2, [(w32_ref, w_ref), (pwo32_ref, pwo_ref)])
        x = _out_epilogue(a_ref, x_ref, pmod_ref, pwo_ref, plng_ref, plnb_ref)
        y_ref[0] = x
    else:
        (w_ref,) = refs
        _cast_weights_once(2, [(w32_ref, w_ref)])
        x = x_ref[0]
    shift, scale, _ = _modulation(mod_ref)
    h = (x * (1.0 + scale) + shift).astype(BF16)
    w = NA_WIDTH
    q_ref[0] = (_dot(h, w_ref[:, :w]) * NA_Q_SCALE).astype(BF16)
    k = _dot(h, w_ref[:, w:2 * w])
    v = _dot(h, w_ref[:, 2 * w:3 * w])
    k_ref[0] = k.astype(BF16)
    v_ref[0] = v.astype(BF16)
    z_ref[0] = _dot(h, w_ref[:, 3 * w:])
    if emit_state:
        sk_ref[...] = k.reshape(sk_ref.shape)
        sv_ref[...] = v.reshape(sv_ref.shape)


def _na_proj(x, mod, slot, w_in, emit_state, prev=None):
    bsz, seq_len, _ = x.shape
    t_rows = min(ROW_TILE, seq_len)
    tok = pl.BlockSpec((1, t_rows, NA_WIDTH), lambda b, i: (b, i, 0))
    w_spec = _weight_spec((1, D_MODEL, 4 * NA_WIDTH), (slot, 0, 0))
    out_specs = [tok, tok, tok, tok]
    act = jax.ShapeDtypeStruct((bsz, seq_len, NA_WIDTH), BF16)
    out_shape = [act, act, act, jax.ShapeDtypeStruct((bsz, seq_len, NA_WIDTH), F32)]
    scratch = [pltpu.VMEM((D_MODEL, 4 * NA_WIDTH), BF16)]
    if emit_state:
        st_spec, st_shape = _state_out(bsz, seq_len, t_rows, emit_state)
        out_specs += [st_spec(NA_WIDTH)] * 2
        out_shape += [st_shape(NA_WIDTH)] * 2
    if prev is None:
        in_specs = [tok, _mod_spec(mod), w_spec]
        args = [x, mod[0], w_in]
    else:
        a, mod_prev, slot_prev, w_out_prev, ln_g_prev, ln_b_prev = prev
        in_specs = [tok, tok, _mod_spec(mod_prev), _weight_spec((1, D_MODEL, D_MODEL), (slot_prev, 0, 0)),
                    _row_spec(ln_g_prev), _row_spec(ln_b_prev), _mod_spec(mod), w_spec]
        args = [a, x, mod_prev[0], w_out_prev, ln_g_prev[0], ln_b_prev[0], mod[0], w_in]
        out_specs.append(tok)
        out_shape.append(jax.ShapeDtypeStruct(x.shape, F32))
        scratch.append(pltpu.VMEM((D_MODEL, D_MODEL), BF16))
    return pl.pallas_call(
        functools.partial(_na_proj_kernel, emit_state, prev is not None),
        grid=(bsz, seq_len // t_rows),
        in_specs=in_specs,
        out_specs=out_specs,
        out_shape=out_shape,
        scratch_shapes=scratch,
        compiler_params=_params(2),
        name="na_proj",
    )(*args)


def _split_head_pair(q2, low_half):
    zero = jnp.zeros_like(q2)
    return jnp.concatenate([jnp.where(low_half, q2, zero), jnp.where(low_half, zero, q2)], axis=0)


def _na_ctx_attn_kernel(q_ref, k_ref, v_ref, z_ref, o_ref):
    n = q_ref.shape[1]
    low_half = lax.broadcasted_iota(jnp.int32, (1, LANES), 1) < NA_HEAD_DIM
    for j in range(NA_PAIRS):
        lanes = slice(j * LANES, (j + 1) * LANES)
        qs = _split_head_pair(q_ref[0, :, lanes], low_half)
        o = _softmax_pv([_dot_nt(qs, k_ref[0, :, lanes])], [v_ref[0, :, lanes]])
        o2 = jnp.where(low_half, o[:n], o[n:])
        o_ref[0, :, lanes] = (o2 * _silu(z_ref[0, :, lanes])).astype(BF16)


def _na_ctx_kernel(q_ref, k_ref, v_ref, z_ref, x_ref, mod_ref, wo32_ref, lng_ref, lnb_ref, y_ref, a_ref, wo_ref):
    _cast_weights_once(1, [(wo32_ref, wo_ref)])
    _na_ctx_attn_kernel(q_ref, k_ref, v_ref, z_ref, a_ref)
    y_ref[0] = _out_epilogue(a_ref, x_ref, mod_ref, wo_ref, lng_ref, lnb_ref)


def _na_ctx_layer(q, k, v, z, x, mod, slot, w_out, ln_g, ln_b):
    bsz, seq_len, _ = x.shape
    tok, tail_specs = _ctx_fused_specs(seq_len, slot, mod, ln_g, ln_b)
    return pl.pallas_call(
        _na_ctx_kernel,
        grid=(bsz,),
        in_specs=[tok, tok, tok, tok] + tail_specs,
        out_specs=tok,
        out_shape=jax.ShapeDtypeStruct(x.shape, F32),
        scratch_shapes=[pltpu.VMEM((1, seq_len, NA_WIDTH), BF16), pltpu.VMEM((D_MODEL, D_MODEL), BF16)],
        compiler_params=_params(1),
        name="na_ctx_layer",
    )(q, k, v, z, x, mod[0], w_out, ln_g[0], ln_b[0])


NA_N_DR = 2 * NA_WIN_R - 1
NA_BIAS_SLOTS = NA_N_DR + 1


def _na_bias_kernel(rpb_ref, o_ref):
    qc = lax.broadcasted_iota(jnp.int32, (GRID_W, LANES), 0)
    lane = lax.broadcasted_iota(jnp.int32, (GRID_W, LANES), 1)
    low_half = lane < GRID_W
    kc = jnp.where(low_half, lane, lane - GRID_W)
    cstart = jnp.clip(qc - NA_WIN_C // 2, 0, GRID_W - NA_WIN_C)
    valid = (kc >= cstart) & (kc < cstart + NA_WIN_C)
    for h in range(rpb_ref.shape[0]):
        for i in range(NA_BIAS_SLOTS):
            row_l = jnp.broadcast_to(rpb_ref[h, pl.ds(max(i - 1, 0), 1), :], (GRID_W, LANES))
            row_r = jnp.broadcast_to(rpb_ref[h, pl.ds(min(i, NA_N_DR - 1), 1), :], (GRID_W, LANES))
            left = pltpu.roll(row_l, LANES - (NA_WIN_C - 1), 1, stride=1, stride_axis=0)
            right = pltpu.roll(row_r, GRID_W - (NA_WIN_C - 1), 1, stride=1, stride_axis=0)
            tile = jnp.where(low_half, left, right)
            o_ref[h, i] = jnp.where(valid, tile * LOG2E, NEG_INF)


def _na_bias_table(rpb):
    n_dr, n_dc = rpb.shape[1:]
    rpb_rows = jnp.pad(rpb, ((0, 0), (0, 16 - n_dr), (0, LANES - n_dc)))
    heads_per_step = 4
    return pl.pallas_call(
        _na_bias_kernel,
        grid=(NA_HEADS // heads_per_step,),
        in_specs=[pl.BlockSpec((heads_per_step, 16, LANES), lambda h: (h, 0, 0))],
        out_specs=pl.BlockSpec((heads_per_step, NA_BIAS_SLOTS, GRID_W, LANES), lambda h: (h, 0, 0, 0)),
        out_shape=jax.ShapeDtypeStruct((NA_HEADS, NA_BIAS_SLOTS, GRID_W, LANES), F32),
        compiler_params=_params(1),
        name="na_bias_table",
    )(rpb_rows)


NA_ROWS_PER_GROUP = 4
NA_BAND_ROWS = NA_ROWS_PER_GROUP + NA_WIN_R
NA_GROUPS_PER_STEP = 8


def _na_lat_attn_kernel(n_rows, n_groups, q_ref, k_ref, v_ref, kc_ref, vc_ref, b_ref, z_ref, o_ref):
    n_q = NA_ROWS_PER_GROUP * GRID_W
    low_half = lax.broadcasted_iota(jnp.int32, (1, LANES), 1) < NA_HEAD_DIM
    for grp in range(n_groups):
        rows = slice(grp * n_q, (grp + 1) * n_q)
        r0 = (pl.program_id(2) * n_groups + grp) * NA_ROWS_PER_GROUP
        band0 = jnp.clip(r0 - NA_WIN_R // 2, 0, n_rows - NA_BAND_ROWS)
        qs = _split_head_pair(q_ref[0, rows, :], low_half)
        key0 = pl.multiple_of(band0 * GRID_W, GRID_W)
        kl = k_ref[0, pl.ds(key0, NA_BAND_ROWS * GRID_W), :]
        vl = v_ref[0, pl.ds(key0, NA_BAND_ROWS * GRID_W), :]
        bias_rows = []
        for rr in range(NA_ROWS_PER_GROUP):
            r = r0 + rr
            win0 = jnp.clip(r - NA_WIN_R // 2, 0, n_rows - NA_WIN_R)
            blocks = []
            for cblk in range(NA_BAND_ROWS // 2):
                key_row = band0 + 2 * cblk
                slot = jnp.clip(key_row - r + NA_WIN_R, 0, NA_BIAS_SLOTS - 1)
                left_pen = jnp.where((key_row >= win0) & (key_row < win0 + NA_WIN_R), 0.0, NEG_INF)
                right_pen = jnp.where((key_row + 1 >= win0) & (key_row + 1 < win0 + NA_WIN_R), 0.0, NEG_INF)
                blocks.append((slot, jnp.where(low_half, left_pen, right_pen)))
            bias_rows.append(blocks)
        bias = jnp.concatenate(
            [jnp.concatenate(
                [jnp.concatenate([b_ref[e, slot] + pen for slot, pen in blocks], axis=1)
                 for blocks in bias_rows], axis=0)
             for e in range(2)], axis=0)
        o = _softmax_pv([_dot_nt(qs, kl) + bias, _dot_nt(qs, kc_ref[0])], [vl, vc_ref[0]])
        o2 = jnp.where(low_half, o[:n_q], o[n_q:])
        o_ref[0, rows, :] = (o2 * _silu(z_ref[0, rows, :])).astype(BF16)


def _na_lat_attention(q, k, v, k_ctx, v_ctx, bias, z):
    bsz, seq_len, _ = q.shape
    n_rows = seq_len // GRID_W
    past = k_ctx.shape[1]
    n_groups = min(NA_GROUPS_PER_STEP, n_rows // NA_ROWS_PER_GROUP)
    rows_per_step = n_groups * NA_ROWS_PER_GROUP
    row_blk = pl.BlockSpec((1, rows_per_step * GRID_W, LANES), lambda b, j, r: (b, r, j))
    seq_blk = pl.BlockSpec((1, seq_len, LANES), lambda b, j, r: (b, 0, j))
    ctx_blk = pl.BlockSpec((1, past, LANES), lambda b, j, r: (b, 0, j))
    return pl.pallas_call(
        functools.partial(_na_lat_attn_kernel, n_rows, n_groups),
        grid=(bsz, NA_PAIRS, n_rows // rows_per_step),
        in_specs=[row_blk, seq_blk, seq_blk, ctx_blk, ctx_blk,
                  pl.BlockSpec((2, NA_BIAS_SLOTS, GRID_W, LANES), lambda b, j, r: (j, 0, 0, 0)),
                  row_blk],
        out_specs=row_blk,
        out_shape=jax.ShapeDtypeStruct((bsz, seq_len, NA_WIDTH), BF16),
        compiler_params=_params(3),
        name="na_lat_attention",
    )(q, k, v, k_ctx, v_ctx, bias, z)


def _rope_swap_perm():
    nf = QK_ROPE // 4
    idx = np.arange(QK_ROPE).reshape(2, 2, nf)
    return idx[:, ::-1, :].reshape(-1)


def _rope_table(seq_len):
    nf = QK_ROPE // 4
    inv = ROPE_BASE ** (-jnp.arange(nf, dtype=F32) / nf)
    t = jnp.arange(seq_len)
    pos = jnp.stack([t // GRID_W, t % GRID_W], -1).astype(F32)
    ang = pos[:, :, None] * inv
    cos = jnp.cos(ang)
    sin = jnp.sin(ang)
    c = jnp.concatenate([cos, cos], axis=-1).reshape(seq_len, QK_ROPE)
    s = jnp.concatenate([-sin, sin], axis=-1).reshape(seq_len, QK_ROPE)
    return jnp.concatenate([c, s], axis=-1)


def _mla_weights(w_in, w_uq):
    perm = _rope_swap_perm()
    o1, o2, o3 = Q_LORA, Q_LORA + KV_LORA, Q_LORA + KV_LORA + QK_ROPE
    w_kr = w_in[:, o2:o3]
    w1 = jnp.concatenate([w_in[:, :o2], w_kr, w_kr[:, perm], w_in[:, o3:]], axis=1).astype(BF16)
    wq = w_uq.reshape(Q_LORA, MLA_HEADS, QK_NOPE + QK_ROPE)
    nope = wq[:, :, :QK_NOPE].reshape(Q_LORA, MLA_HEADS * QK_NOPE)
    rope = wq[:, :, QK_NOPE:]
    rope_blk = jnp.concatenate([rope, rope[:, :, perm]], axis=-1).reshape(Q_LORA, MLA_HEADS * LANES)
    return w1, jnp.concatenate([nope, rope_blk], axis=1).astype(BF16)


def kernel(x_prompt, x_sample, cache_mla_ckv, cache_mla_krope, cache_na_k, cache_na_v, c, c_ctx, ada_w, ada_b, ln_g, ln_b, pool_w_in, pool_w_grp, pool_scale, pool_w_out, mla_w_in, mla_q_norm, mla_w_uq, mla_kv_norm, mla_w_ukv, mla_w_out, na_w_in, na_rpb, na_w_out):
    n_ctx, n_lat = x_prompt.shape[0], x_sample.shape[0]
    lat_len = x_sample.shape[1]
    cond = jnp.concatenate([c_ctx[None], c, jnp.zeros((8 - 1 - n_lat, D_MODEL), F32)], axis=0)
    mod = _ada(cond, ada_w, ada_b)
    yp, ys = x_prompt, x_sample
    st_ckv = st_kr = st_k = st_v = None
    ctx_len = x_prompt.shape[1]
    ctx_fold = 2 if (n_ctx % 2 == 0 and 2 * ctx_len <= ROW_TILE) else 1
    fold = lambda t: t.reshape((n_ctx // ctx_fold, ctx_fold * ctx_len) + t.shape[2:])
    unfold = lambda t: t.reshape((n_ctx, ctx_len) + t.shape[2:])
    pending = None
    mod_ctx_rows = mod[:, 0:1]
    mod_lat_rows = mod[:, 1:1 + n_lat].reshape(DEPTH * n_lat, 1, 3 * D_MODEL)
    ln_g_rows, ln_b_rows = ln_g[:, None, :], ln_b[:, None, :]
    pool_scale_rows = pool_scale[:, None, :]
    for i in range(DEPTH):
        kind, j = i % N_MIXERS, i // N_MIXERS
        mod_p = mod_pf = (mod_ctx_rows, i, False)
        mod_s = (mod_lat_rows, i * n_lat, True)
        g, b = (ln_g_rows, i), (ln_b_rows, i)
        if kind == 1 and pending is not None:
            ys = _out_layer(pending[0], ys, *pending[1:])
            pending = None
        if kind == 0:
            ps = (pool_scale_rows, j)
            yp = unfold(_pool_layer(fold(yp), ctx_len, mod_pf, j, pool_w_in, pool_w_grp, ps, pool_w_out, g, b))
            ys = _pool_layer(ys, lat_len, mod_s, j, pool_w_in, pool_w_grp, ps, pool_w_out, g, b, prev=pending)
            pending = None
        elif kind == 1:
            w1, w_uq = _mla_weights(mla_w_in[j], mla_w_uq[j])
            w_ukv = mla_w_ukv[j].astype(BF16)
            qn, kvn = mla_q_norm[j][None], mla_kv_norm[j][None]
            q_p, kv_p, krp_p, z_p, st_ckv, st_kr = _mla_proj(fold(yp), mod_pf, w1, qn, kvn, w_uq, w_ukv, None, ctx_fold)
            yp = _mla_ctx_layer(unfold(q_p), unfold(z_p), unfold(kv_p), unfold(krp_p), yp, mod_p, j, mla_w_out, g, b)
            q_s, kv_s, krp_s, z_s = _mla_proj(ys, mod_s, w1, qn, kvn, w_uq, w_ukv, _rope_table(lat_len), False)
            kv_c, krp_c = _mla_cache_keys(cache_mla_ckv, cache_mla_krope, j, w_ukv)
            a_s = _mla_attention(q_s, z_s, [(kv_c, krp_c), (kv_s, krp_s)], lat_len, 2, 2 * ATTN_ROWS)
            pending = (a_s, mod_s, j, mla_w_out, g, b)
        else:
            q_p, k_p, v_p, z_p, st_k, st_v = _na_proj(fold(yp), mod_pf, j, na_w_in, ctx_fold)
            yp = _na_ctx_layer(unfold(q_p), unfold(k_p), unfold(v_p), unfold(z_p), yp, mod_p, j, na_w_out, g, b)
            if pending is None:
                q_s, k_s, v_s, z_s = _na_proj(ys, mod_s, j, na_w_in, False)
            else:
                q_s, k_s, v_s, z_s, ys = _na_proj(ys, mod_s, j, na_w_in, False, prev=pending)
                pending = None
            past = cache_na_k.shape[2]
            k_c = cache_na_k[:, j].reshape(n_lat, past, NA_WIDTH).astype(BF16)
            v_c = cache_na_v[:, j].reshape(n_lat, past, NA_WIDTH).astype(BF16)
            bias = _na_bias_table(na_rpb[j])
            a_s = _na_lat_attention(q_s, k_s, v_s, k_c, v_c, bias, z_s)
            pending = (a_s, mod_s, j, na_w_out, g, b)
    if pending is not None:
        ys = _out_layer(pending[0], ys, *pending[1:])
    state_na_k = st_k.reshape(n_ctx, 1, -1, NA_HEADS, NA_HEAD_DIM)
    state_na_v = st_v.reshape(n_ctx, 1, -1, NA_HEADS, NA_HEAD_DIM)
    return (yp, ys, st_ckv, st_kr, state_na_k, state_na_v)
```

```python
import functools
import math

import numpy as np
import jax
import jax.numpy as jnp
from jax import lax
from jax.experimental import pallas as pl
from jax.experimental.pallas import tpu as pltpu

F32 = jnp.float32
BF16 = jnp.bfloat16

D_MODEL = 1024
DEPTH = 4
GRID_W = 64
N_MIXERS = 3

POOL_WINDOWS = (2, 4, 8, 16)
POOL_GROUP_DIM = D_MODEL // len(POOL_WINDOWS)
POOL_HALO = 8

MLA_HEADS = 8
QK_NOPE = 128
QK_ROPE = 64
V_HEAD = 128
Q_LORA = D_MODEL // 2
KV_LORA = D_MODEL // 4
MLA_WIDTH = MLA_HEADS * V_HEAD
MLA_SCALE = (QK_NOPE + QK_ROPE) ** -0.5

NA_HEADS = 16
NA_HEAD_DIM = 64
NA_WIDTH = NA_HEADS * NA_HEAD_DIM
NA_WIN_R = 8
NA_WIN_C = 16
NA_SCALE = NA_HEAD_DIM ** -0.5
NA_PAIRS = NA_HEADS // 2

ROPE_BASE = 10000.0
LN_EPS = 1e-5
RMS_EPS = 1e-6
NEG_INF = -1e30
DEEPNORM_ALPHA = (2 * DEPTH) ** 0.25
LOG2E = math.log2(math.e)

LANES = 128
ROW_TILE = 512
ATTN_ROWS = 256
VMEM_LIMIT = 56 * 1024 * 1024


def _params(n_axes):
    return pltpu.CompilerParams(dimension_semantics=("arbitrary",) * n_axes, vmem_limit_bytes=VMEM_LIMIT)


def _silu(x):
    h = 0.5 * x
    return h + h * jnp.tanh(h)


def _dot(a, b):
    return jnp.dot(a, b, preferred_element_type=F32)


def _dot_nt(a, b):
    return lax.dot_general(a, b, (((1,), (1,)), ((), ())), preferred_element_type=F32)


def _weight_spec(block_shape, index):
    return pl.BlockSpec(block_shape, lambda *_: index, pipeline_mode=pl.Buffered(1))


def _mod_spec(mod):
    _, first_row, per_batch = mod
    return pl.BlockSpec((1, 1, 3 * D_MODEL), lambda b, *_: (first_row + b if per_batch else first_row, 0, 0))


def _row_spec(param):
    rows, row = param
    return pl.BlockSpec((None, 1, rows.shape[2]), lambda *_: (row, 0, 0))


def _cast_weights_once(n_axes, pairs):
    first = functools.reduce(jnp.logical_and, [pl.program_id(a) == 0 for a in range(n_axes)])

    @pl.when(first)
    def _():
        for src, dst in pairs:
            dst[...] = src[...].reshape(dst.shape).astype(BF16)


def _state_out(bsz, seq_len, t_rows, n_seq):
    assert t_rows == seq_len and seq_len % n_seq == 0
    spec = lambda width: pl.BlockSpec((n_seq, 1, seq_len // n_seq, width), lambda b, i: (b, 0, 0, 0))
    shape = lambda width: jax.ShapeDtypeStruct((bsz * n_seq, 1, seq_len // n_seq, width), F32)
    return spec, shape


def _modulation(mod_ref):
    m = mod_ref[0]
    return m[:, :D_MODEL], m[:, D_MODEL:2 * D_MODEL], m[:, 2 * D_MODEL:]


def _deepnorm_ln(x, gate, branch, g, b):
    xf = DEEPNORM_ALPHA * x + gate * branch
    mu = jnp.mean(xf, axis=-1, keepdims=True)
    xc = xf - mu
    var = jnp.mean(xc * xc, axis=-1, keepdims=True)
    return xc * lax.rsqrt(var + LN_EPS) * g + b


def _rms(x, g):
    return x * lax.rsqrt(jnp.mean(x * x, axis=-1, keepdims=True) + RMS_EPS) * g


def _ada_kernel(cond_ref, w_ref, b_ref, o_ref):
    c = cond_ref[...]
    o_ref[0] = _dot(_silu(c).astype(BF16), w_ref[0].astype(BF16)) + b_ref[0]


def _ada(cond, ada_w, ada_b):
    tn = D_MODEL
    return pl.pallas_call(
        _ada_kernel,
        grid=(DEPTH, 3 * D_MODEL // tn),
        in_specs=[
            pl.BlockSpec((8, D_MODEL), lambda l, n: (0, 0)),
            pl.BlockSpec((1, D_MODEL, tn), lambda l, n: (l, 0, n)),
            pl.BlockSpec((1, 1, tn), lambda l, n: (l, 0, n)),
        ],
        out_specs=pl.BlockSpec((1, 8, tn), lambda l, n: (l, 0, n)),
        out_shape=jax.ShapeDtypeStruct((DEPTH, 8, 3 * D_MODEL), F32),
        compiler_params=_params(2),
        name="ada",
    )(cond, ada_w, ada_b.reshape(DEPTH, 1, 3 * D_MODEL))


def _rows_ahead(x, k):
    return x if k == 0 else pltpu.roll(x, x.shape[0] - k, 0)


def _pool_chain(seq_len, t0, x, x_prev, x_next, mods, weights, ln):
    c_rows = x.shape[0]
    shift, scale, gate = mods
    wu_ref, wz_ref, wg_ref, ps_ref, wo_ref = weights
    one_scale = 1.0 + scale
    hm = x * one_scale + shift
    hp = x_prev * one_scale + shift
    hn = x_next * one_scale + shift
    hext = jnp.concatenate([hm, hp, hn], axis=0).astype(BF16)
    u = _dot(hext, wu_ref[...])
    z = _dot(hext[:c_rows], wz_ref[...])
    prev_ok = (t0 > 0).astype(F32)
    next_ok = (t0 + c_rows < seq_len).astype(F32)
    centre = u[:c_rows]
    useq = jnp.concatenate([u[c_rows:c_rows + POOL_HALO] * prev_ok, centre, u[c_rows + POOL_HALO:] * next_ok], axis=0)

    t = t0 + lax.broadcasted_iota(jnp.int32, (c_rows, 1), 0)
    mixed = []
    for g, w in enumerate(POOL_WINDOWS):
        cols = slice(g * POOL_GROUP_DIM, (g + 1) * POOL_GROUP_DIM)
        q = useq[:, cols]
        span = 1
        while span < w // 2:
            q = q + _rows_ahead(q, span)
            span *= 2
        acc = _rows_ahead(q, POOL_HALO - w // 2)[:c_rows] + q[POOL_HALO:POOL_HALO + c_rows]
        lo = jnp.clip(t - w // 2, 0, seq_len)
        hi = jnp.clip(t + (w - w // 2), 0, seq_len)
        inv_cnt = 1.0 / (hi - lo).astype(F32)
        mg = (acc * inv_cnt - centre[:, cols]).astype(BF16)
        mixed.append(_dot(mg, wg_ref[g]))
    mixed = jnp.concatenate(mixed, axis=1) * ps_ref[...]
    a = (mixed * _silu(z)).astype(BF16)
    branch = _dot(a, wo_ref[...])
    return _deepnorm_ln(x, gate, branch, *ln)


POOL_A_HALO = 16


def _pool_kernel(seq_len, fuse_prev, *refs):
    if fuse_prev:
        a_ref, ap_ref, an_ref, pmod_ref, pwo32_ref, plng_ref, plnb_ref = refs[:7]
        refs = refs[7:]
    x_ref, xp_ref, xn_ref, mod_ref, wu32_ref, wz32_ref, wg32_ref, ps_ref, wo32_ref, lng_ref, lnb_ref, o_ref = refs[:12]
    wu_ref, wz_ref, wg_ref, wo_ref = refs[12:16]
    casts = [(wu32_ref, wu_ref), (wz32_ref, wz_ref), (wg32_ref, wg_ref), (wo32_ref, wo_ref)]
    if fuse_prev:
        pwo_ref = refs[16]
        casts.append((pwo32_ref, pwo_ref))
    _cast_weights_once(2, casts)
    t_rows = x_ref.shape[1]
    c_rows = min(t_rows, seq_len)
    n_chain = t_rows // c_rows
    mods = _modulation(mod_ref)
    weights = (wu_ref, wz_ref, wg_ref, ps_ref, wo_ref)
    ln = (lng_ref[...], lnb_ref[...])
    x_all, x_before, x_after = x_ref[0], xp_ref[0], xn_ref[0]
    if fuse_prev:
        _, _, pgate = _modulation(pmod_ref)
        pln = (plng_ref[...], plnb_ref[...])
        x_all = _deepnorm_ln(x_all, pgate, _dot(a_ref[0], pwo_ref[...]), *pln)
        x_before = _deepnorm_ln(x_before, pgate, _dot(ap_ref[0], pwo_ref[...])[POOL_A_HALO - POOL_HALO:], *pln)
        x_after = _deepnorm_ln(x_after, pgate, _dot(an_ref[0], pwo_ref[...])[:POOL_HALO], *pln)
    for ci in range(n_chain):
        r0 = ci * c_rows
        t0 = lax.rem(pl.program_id(1) * t_rows + r0, seq_len)
        x_prev = x_before if ci == 0 else x_all[r0 - POOL_HALO:r0]
        x_next = x_after if ci == n_chain - 1 else x_all[r0 + c_rows:r0 + c_rows + POOL_HALO]
        o_ref[0, r0:r0 + c_rows, :] = _pool_chain(seq_len, t0, x_all[r0:r0 + c_rows], x_prev, x_next,
                                                  mods, weights, ln)


def _pool_layer(x, seq_len, mod, slot, w_in, w_grp, p_scale, w_out, ln_g, ln_b, prev=None):
    bsz, row_len, _ = x.shape
    n_grp = len(POOL_WINDOWS)
    t_rows = min(ROW_TILE, row_len)
    assert row_len % seq_len == 0 and (seq_len % t_rows == 0 or t_rows % seq_len == 0)
    n_tiles = row_len // t_rows
    xh = x.reshape(bsz * row_len // POOL_HALO, POOL_HALO, D_MODEL)

    def halo_specs(rows):
        per_tile, per_seq = t_rows // rows, row_len // rows
        last = bsz * per_seq - 1
        return [pl.BlockSpec((1, rows, D_MODEL), lambda b, i: (jnp.maximum(b * per_seq + i * per_tile - 1, 0), 0, 0)),
                pl.BlockSpec((1, rows, D_MODEL), lambda b, i: (jnp.minimum(b * per_seq + (i + 1) * per_tile, last), 0, 0))]

    tok = pl.BlockSpec((1, t_rows, D_MODEL), lambda b, i: (b, i, 0))
    in_specs = [tok] + halo_specs(POOL_HALO) + [
        _mod_spec(mod),
        _weight_spec((1, D_MODEL, D_MODEL), (slot, 0, 0)),
        _weight_spec((1, D_MODEL, D_MODEL), (slot, 0, 1)),
        _weight_spec((1, n_grp, POOL_GROUP_DIM, POOL_GROUP_DIM), (slot, 0, 0, 0)),
        _row_spec(p_scale),
        _weight_spec((1, D_MODEL, D_MODEL), (slot, 0, 0)),
        _row_spec(ln_g),
        _row_spec(ln_b)]
    args = [x, xh, xh, mod[0], w_in, w_in, w_grp, p_scale[0], w_out, ln_g[0], ln_b[0]]
    scratch = [pltpu.VMEM((D_MODEL, D_MODEL), BF16),
               pltpu.VMEM((D_MODEL, D_MODEL), BF16),
               pltpu.VMEM((n_grp, POOL_GROUP_DIM, POOL_GROUP_DIM), BF16),
               pltpu.VMEM((D_MODEL, D_MODEL), BF16)]
    if prev is not None:
        a, mod_prev, slot_prev, w_out_prev, ln_g_prev, ln_b_prev = prev
        ah = a.reshape(bsz * row_len // POOL_A_HALO, POOL_A_HALO, D_MODEL)
        in_specs = [tok] + halo_specs(POOL_A_HALO) + [
            _mod_spec(mod_prev), _weight_spec((1, D_MODEL, D_MODEL), (slot_prev, 0, 0)),
            _row_spec(ln_g_prev), _row_spec(ln_b_prev)] + in_specs
        args = [a, ah, ah, mod_prev[0], w_out_prev, ln_g_prev[0], ln_b_prev[0]] + args
        scratch.append(pltpu.VMEM((D_MODEL, D_MODEL), BF16))
    return pl.pallas_call(
        functools.partial(_pool_kernel, seq_len, prev is not None),
        grid=(bsz, n_tiles),
        in_specs=in_specs,
        out_specs=tok,
        out_shape=jax.ShapeDtypeStruct(x.shape, F32),
        scratch_shapes=scratch,
        compiler_params=_params(2),
        name="pool_layer",
    )(*args)


def _out_epilogue(a_ref, x_ref, mod_ref, wo_ref, lng_ref, lnb_ref):
    _, _, gate = _modulation(mod_ref)
    branch = _dot(a_ref[0], wo_ref[...])
    return _deepnorm_ln(x_ref[0], gate, branch, lng_ref[...], lnb_ref[...])


def _out_kernel(a_ref, x_ref, mod_ref, wo32_ref, lng_ref, lnb_ref, o_ref, wo_ref):
    _cast_weights_once(2, [(wo32_ref, wo_ref)])
    o_ref[0] = _out_epilogue(a_ref, x_ref, mod_ref, wo_ref, lng_ref, lnb_ref)


def _out_layer(a, x, mod, slot, w_out, ln_g, ln_b):
    bsz, seq_len, _ = x.shape
    t_rows = min(ROW_TILE, seq_len)
    const2 = lambda b, i: (0, 0)
    tok = pl.BlockSpec((1, t_rows, D_MODEL), lambda b, i: (b, i, 0))
    return pl.pallas_call(
        _out_kernel,
        grid=(bsz, seq_len // t_rows),
        in_specs=[tok, tok, _mod_spec(mod), _weight_spec((1, D_MODEL, D_MODEL), (slot, 0, 0)),
                  _row_spec(ln_g), _row_spec(ln_b)],
        out_specs=tok,
        out_shape=jax.ShapeDtypeStruct(x.shape, F32),
        scratch_shapes=[pltpu.VMEM((D_MODEL, D_MODEL), BF16)],
        compiler_params=_params(2),
        name="out_layer",
    )(a, x, mod[0], w_out, ln_g[0], ln_b[0])


MLA_W1_COLS = Q_LORA + KV_LORA + 2 * QK_ROPE + MLA_WIDTH
MLA_Z_OFF = Q_LORA + KV_LORA + 2 * QK_ROPE
MLA_Q_COLS = MLA_HEADS * QK_NOPE + MLA_HEADS * LANES
MLA_Q_SCALE = MLA_SCALE * LOG2E


def _rope_block(blk, cs, low_half):
    if cs is not None:
        blk = blk * cs
        blk = blk + pltpu.roll(blk, LANES // 2, 1)
    return jnp.where(low_half, blk, 0.0)


def _mla_proj_kernel(rope, emit_state, *refs):
    x_ref, mod_ref, w1_ref, qn_ref, kvn_ref, wuq_ref, wukv_ref = refs[:7]
    refs = refs[7:]
    cs = None
    if rope:
        cs = refs[0][...]
        refs = refs[1:]
    q_ref, kv_ref, krp_ref, z_ref = refs[:4]
    shift, scale, _ = _modulation(mod_ref)
    h = (x_ref[0] * (1.0 + scale) + shift).astype(BF16)
    p = _dot(h, w1_ref[...])
    cq = p[:, :Q_LORA]
    ckv = p[:, Q_LORA:Q_LORA + KV_LORA]
    krb = p[:, Q_LORA + KV_LORA:MLA_Z_OFF]
    z_ref[0] = p[:, MLA_Z_OFF:].astype(BF16)
    ckvn = _rms(ckv, kvn_ref[...])
    if emit_state:
        ckv_ref, kr_ref = refs[4:6]
        ckv_ref[...] = ckvn.reshape(ckv_ref.shape)
        kr_ref[...] = krb[:, :QK_ROPE].reshape(kr_ref.shape)
    low_half = lax.broadcasted_iota(jnp.int32, (1, LANES), 1) < LANES // 2
    krp_ref[0] = _rope_block(krb, cs, low_half).astype(BF16)
    kv_ref[0] = _dot(ckvn.astype(BF16), wukv_ref[...]).astype(BF16)
    qf = _dot(_rms(cq, qn_ref[...]).astype(BF16), wuq_ref[...])
    n_nope = MLA_HEADS * QK_NOPE
    q_ref[0, :, :n_nope] = (qf[:, :n_nope] * MLA_Q_SCALE).astype(BF16)
    for hh in range(MLA_HEADS):
        cols = slice(n_nope + hh * LANES, n_nope + (hh + 1) * LANES)
        q_ref[0, :, cols] = (_rope_block(qf[:, cols], cs, low_half) * MLA_Q_SCALE).astype(BF16)


def _mla_proj(x, mod, w1, q_norm, kv_norm, w_uq, w_ukv, rope_cs, emit_state):
    bsz, seq_len, _ = x.shape
    t_rows = min(ROW_TILE, seq_len)
    const2 = lambda b, i: (0, 0)
    tok = lambda width: pl.BlockSpec((1, t_rows, width), lambda b, i: (b, i, 0))
    in_specs = [tok(D_MODEL),
                _mod_spec(mod),
                pl.BlockSpec((D_MODEL, MLA_W1_COLS), const2),
                pl.BlockSpec((1, Q_LORA), const2),
                pl.BlockSpec((1, KV_LORA), const2),
                pl.BlockSpec((Q_LORA, MLA_Q_COLS), const2),
                pl.BlockSpec((KV_LORA, MLA_HEADS * (QK_NOPE + V_HEAD)), const2)]
    args = [x, mod[0], w1, q_norm, kv_norm, w_uq, w_ukv]
    if rope_cs is not None:
        in_specs.append(pl.BlockSpec((t_rows, LANES), lambda b, i: (i, 0)))
        args.append(rope_cs)
    out_specs = [tok(MLA_Q_COLS), tok(MLA_HEADS * (QK_NOPE + V_HEAD)), tok(LANES), tok(MLA_WIDTH)]
    out_shape = [jax.ShapeDtypeStruct((bsz, seq_len, MLA_Q_COLS), BF16),
                 jax.ShapeDtypeStruct((bsz, seq_len, MLA_HEADS * (QK_NOPE + V_HEAD)), BF16),
                 jax.ShapeDtypeStruct((bsz, seq_len, LANES), BF16),
                 jax.ShapeDtypeStruct((bsz, seq_len, MLA_WIDTH), BF16)]
    if emit_state:
        st_spec, st_shape = _state_out(bsz, seq_len, t_rows, emit_state)
        out_specs += [st_spec(KV_LORA), st_spec(QK_ROPE)]
        out_shape += [st_shape(KV_LORA), st_shape(QK_ROPE)]
    return pl.pallas_call(
        functools.partial(_mla_proj_kernel, rope_cs is not None, emit_state),
        grid=(bsz, seq_len // t_rows),
        in_specs=in_specs,
        out_specs=out_specs,
        out_shape=out_shape,
        compiler_params=_params(2),
        name="mla_proj",
    )(*args)


def _mla_cache_kernel(ckv_ref, kr_ref, wukv_ref, kv_ref, krp_ref):
    kv_ref[0] = _dot(ckv_ref[0, 0].astype(BF16), wukv_ref[...]).astype(BF16)
    krp_ref[0, :, :QK_ROPE] = kr_ref[0, 0].astype(BF16)
    krp_ref[0, :, QK_ROPE:] = jnp.zeros((kr_ref.shape[2], LANES - QK_ROPE), BF16)


def _mla_cache_keys(cache_ckv, cache_kr, slot, w_ukv):
    bsz, _, past, _ = cache_ckv.shape
    kv_cols = MLA_HEADS * (QK_NOPE + V_HEAD)
    return pl.pallas_call(
        _mla_cache_kernel,
        grid=(bsz,),
        in_specs=[pl.BlockSpec((1, 1, past, KV_LORA), lambda b: (b, slot, 0, 0)),
                  pl.BlockSpec((1, 1, past, QK_ROPE), lambda b: (b, slot, 0, 0)),
                  pl.BlockSpec((KV_LORA, kv_cols), lambda b: (0, 0))],
        out_specs=[pl.BlockSpec((1, past, kv_cols), lambda b: (b, 0, 0)),
                   pl.BlockSpec((1, past, LANES), lambda b: (b, 0, 0))],
        out_shape=[jax.ShapeDtypeStruct((bsz, past, kv_cols), BF16),
                   jax.ShapeDtypeStruct((bsz, past, LANES), BF16)],
        compiler_params=_params(1),
        name="mla_cache_keys",
    )(cache_ckv, cache_kr, w_ukv)


def _softmax_pv(scores, values):
    m = functools.reduce(jnp.maximum, [jnp.max(s, axis=1, keepdims=True) for s in scores])
    denom = 0.0
    acc = 0.0
    for s, v in zip(scores, values):
        p = jnp.exp2(s - m)
        denom = denom + jnp.sum(p, axis=1, keepdims=True)
        acc = acc + _dot(p.astype(BF16), v)
    return acc / denom


def _mla_attn_kernel(n_seg, heads, sub_rows, qn_ref, qr_ref, z_ref, *refs):
    o_ref = refs[2 * n_seg]
    q_rows = qn_ref.shape[1]
    kv_w = QK_NOPE + V_HEAD
    for h in range(heads):
        lanes = slice(h * LANES, (h + 1) * LANES)
        keys, values = [], []
        for s in range(n_seg):
            kv_ref, kr_ref = refs[2 * s:2 * s + 2]
            keys.append(jnp.concatenate([kv_ref[0, :, h * kv_w:h * kv_w + QK_NOPE], kr_ref[0]], axis=1))
            values.append(kv_ref[0, :, h * kv_w + QK_NOPE:(h + 1) * kv_w])
        for r0 in range(0, q_rows, sub_rows):
            rows = slice(r0, r0 + sub_rows)
            q = jnp.concatenate([qn_ref[0, rows, lanes], qr_ref[0, rows, lanes]], axis=1)
            o = _softmax_pv([_dot_nt(q, k) for k in keys], values)
            o_ref[0, rows, lanes] = (o * _silu(z_ref[0, rows, lanes].astype(F32))).astype(BF16)


def _mla_attention(q, z, segments, q_tile, heads, sub_rows):
    bsz, lq, _ = q.shape
    n_hblk = MLA_HEADS // heads
    hw = heads * LANES
    in_specs = [pl.BlockSpec((1, q_tile, hw), lambda b, h, i: (b, i, h)),
                pl.BlockSpec((1, q_tile, hw), lambda b, h, i: (b, i, n_hblk + h)),
                pl.BlockSpec((1, q_tile, hw), lambda b, h, i: (b, i, h))]
    args = [q, q, z]
    for kv, krp in segments:
        lk = kv.shape[1]
        in_specs += [pl.BlockSpec((1, lk, 2 * hw), lambda b, h, i: (b, 0, h)),
                     pl.BlockSpec((1, lk, LANES), lambda b, h, i: (b, 0, 0))]
        args += [kv, krp]
    return pl.pallas_call(
        functools.partial(_mla_attn_kernel, len(segments), heads, sub_rows),
        grid=(bsz, n_hblk, lq // q_tile),
        in_specs=in_specs,
        out_specs=pl.BlockSpec((1, q_tile, hw), lambda b, h, i: (b, i, h)),
        out_shape=jax.ShapeDtypeStruct((bsz, lq, MLA_WIDTH), BF16),
        compiler_params=_params(3),
        name="mla_attention",
    )(*args)


def _mla_ctx_kernel(qn_ref, qr_ref, z_ref, kv_ref, kr_ref, x_ref, mod_ref, wo32_ref, lng_ref, lnb_ref, y_ref,
                    a_ref, wo_ref):
    _cast_weights_once(1, [(wo32_ref, wo_ref)])
    _mla_attn_kernel(1, MLA_HEADS, ATTN_ROWS, qn_ref, qr_ref, z_ref, kv_ref, kr_ref, a_ref)
    y_ref[0] = _out_epilogue(a_ref, x_ref, mod_ref, wo_ref, lng_ref, lnb_ref)


def _ctx_fused_specs(seq_len, slot, mod, ln_g, ln_b):
    tok = pl.BlockSpec((1, seq_len, D_MODEL), lambda b: (b, 0, 0))
    return tok, [tok, _mod_spec(mod), _weight_spec((1, D_MODEL, D_MODEL), (slot, 0, 0)),
                 _row_spec(ln_g), _row_spec(ln_b)]


def _mla_ctx_layer(q, z, kv, krp, x, mod, slot, w_out, ln_g, ln_b):
    bsz, seq_len, _ = x.shape
    n_nope = MLA_HEADS * QK_NOPE
    tok, tail_specs = _ctx_fused_specs(seq_len, slot, mod, ln_g, ln_b)
    return pl.pallas_call(
        _mla_ctx_kernel,
        grid=(bsz,),
        in_specs=[pl.BlockSpec((1, seq_len, n_nope), lambda b: (b, 0, 0)),
                  pl.BlockSpec((1, seq_len, n_nope), lambda b: (b, 0, 1)),
                  tok,
                  pl.BlockSpec((1, seq_len, kv.shape[2]), lambda b: (b, 0, 0)),
                  pl.BlockSpec((1, seq_len, LANES), lambda b: (b, 0, 0))] + tail_specs,
        out_specs=tok,
        out_shape=jax.ShapeDtypeStruct(x.shape, F32),
        scratch_shapes=[pltpu.VMEM((1, seq_len, MLA_WIDTH), BF16), pltpu.VMEM((D_MODEL, D_MODEL), BF16)],
        compiler_params=_params(1),
        name="mla_ctx_layer",
    )(q, q, z, kv, krp, x, mod[0], w_out, ln_g[0], ln_b[0])


NA_Q_SCALE = NA_SCALE * LOG2E


def _na_proj_kernel(emit_state, fuse_prev, *refs):
    if fuse_prev:
        a_ref, x_ref, pmod_ref, pwo32_ref, plng_ref, plnb_ref, mod_ref, w32_ref = refs[:8]
        refs = refs[8:]
    else:
        x_ref, mod_ref, w32_ref = refs[:3]
        refs = refs[3:]
    q_ref, k_ref, v_ref, z_ref = refs[:4]
    refs = refs[4:]
    if emit_state:
        sk_ref, sv_ref = refs[:2]
        refs = refs[2:]
    if fuse_prev:
        y_ref, w_ref, pwo_ref = refs
        _cast_weights_once(2, [(w32_ref, w_ref), (pwo32_ref, pwo_ref)])
        x = _out_epilogue(a_ref, x_ref, pmod_ref, pwo_ref, plng_ref, plnb_ref)
        y_ref[0] = x
    else:
        (w_ref,) = refs
        _cast_weights_once(2, [(w32_ref, w_ref)])
        x = x_ref[0]
    shift, scale, _ = _modulation(mod_ref)
    h = (x * (1.0 + scale) + shift).astype(BF16)
    w = NA_WIDTH
    q_ref[0] = (_dot(h, w_ref[:, :w]) * NA_Q_SCALE).astype(BF16)
    k = _dot(h, w_ref[:, w:2 * w])
    v = _dot(h, w_ref[:, 2 * w:3 * w])
    k_ref[0] = k.astype(BF16)
    v_ref[0] = v.astype(BF16)
    z_ref[0] = _dot(h, w_ref[:, 3 * w:]).astype(BF16)
    if emit_state:
        sk_ref[...] = k.reshape(sk_ref.shape)
        sv_ref[...] = v.reshape(sv_ref.shape)


def _na_proj(x, mod, slot, w_in, emit_state, prev=None):
    bsz, seq_len, _ = x.shape
    t_rows = min(ROW_TILE, seq_len)
    tok = pl.BlockSpec((1, t_rows, NA_WIDTH), lambda b, i: (b, i, 0))
    w_spec = _weight_spec((1, D_MODEL, 4 * NA_WIDTH), (slot, 0, 0))
    out_specs = [tok, tok, tok, tok]
    act = jax.ShapeDtypeStruct((bsz, seq_len, NA_WIDTH), BF16)
    out_shape = [act, act, act, act]
    scratch = [pltpu.VMEM((D_MODEL, 4 * NA_WIDTH), BF16)]
    if emit_state:
        st_spec, st_shape = _state_out(bsz, seq_len, t_rows, emit_state)
        out_specs += [st_spec(NA_WIDTH)] * 2
        out_shape += [st_shape(NA_WIDTH)] * 2
    if prev is None:
        in_specs = [tok, _mod_spec(mod), w_spec]
        args = [x, mod[0], w_in]
    else:
        a, mod_prev, slot_prev, w_out_prev, ln_g_prev, ln_b_prev = prev
        in_specs = [tok, tok, _mod_spec(mod_prev), _weight_spec((1, D_MODEL, D_MODEL), (slot_prev, 0, 0)),
                    _row_spec(ln_g_prev), _row_spec(ln_b_prev), _mod_spec(mod), w_spec]
        args = [a, x, mod_prev[0], w_out_prev, ln_g_prev[0], ln_b_prev[0], mod[0], w_in]
        out_specs.append(tok)
        out_shape.append(jax.ShapeDtypeStruct(x.shape, F32))
        scratch.append(pltpu.VMEM((D_MODEL, D_MODEL), BF16))
    return pl.pallas_call(
        functools.partial(_na_proj_kernel, emit_state, prev is not None),
        grid=(bsz, seq_len // t_rows),
        in_specs=in_specs,
        out_specs=out_specs,
        out_shape=out_shape,
        scratch_shapes=scratch,
        compiler_params=_params(2),
        name="na_proj",
    )(*args)


def _split_head_pair(q2, low_half):
    zero = jnp.zeros_like(q2)
    return jnp.concatenate([jnp.where(low_half, q2, zero), jnp.where(low_half, zero, q2)], axis=0)


def _na_ctx_attn_kernel(q_ref, k_ref, v_ref, z_ref, o_ref):
    n = q_ref.shape[1]
    low_half = lax.broadcasted_iota(jnp.int32, (1, LANES), 1) < NA_HEAD_DIM
    for j in range(NA_PAIRS):
        lanes = slice(j * LANES, (j + 1) * LANES)
        qs = _split_head_pair(q_ref[0, :, lanes], low_half)
        o = _softmax_pv([_dot_nt(qs, k_ref[0, :, lanes])], [v_ref[0, :, lanes]])
        o2 = jnp.where(low_half, o[:n], o[n:])
        o_ref[0, :, lanes] = (o2 * _silu(z_ref[0, :, lanes].astype(F32))).astype(BF16)


def _na_ctx_kernel(q_ref, k_ref, v_ref, z_ref, x_ref, mod_ref, wo32_ref, lng_ref, lnb_ref, y_ref, a_ref, wo_ref):
    _cast_weights_once(1, [(wo32_ref, wo_ref)])
    _na_ctx_attn_kernel(q_ref, k_ref, v_ref, z_ref, a_ref)
    y_ref[0] = _out_epilogue(a_ref, x_ref, mod_ref, wo_ref, lng_ref, lnb_ref)


def _na_ctx_layer(q, k, v, z, x, mod, slot, w_out, ln_g, ln_b):
    bsz, seq_len, _ = x.shape
    tok, tail_specs = _ctx_fused_specs(seq_len, slot, mod, ln_g, ln_b)
    return pl.pallas_call(
        _na_ctx_kernel,
        grid=(bsz,),
        in_specs=[tok, tok, tok, tok] + tail_specs,
        out_specs=tok,
        out_shape=jax.ShapeDtypeStruct(x.shape, F32),
        scratch_shapes=[pltpu.VMEM((1, seq_len, NA_WIDTH), BF16), pltpu.VMEM((D_MODEL, D_MODEL), BF16)],
        compiler_params=_params(1),
        name="na_ctx_layer",
    )(q, k, v, z, x, mod[0], w_out, ln_g[0], ln_b[0])


NA_N_DR = 2 * NA_WIN_R - 1
NA_BIAS_SLOTS = NA_N_DR + 1


def _na_bias_kernel(rpb_ref, o_ref):
    qc = lax.broadcasted_iota(jnp.int32, (GRID_W, LANES), 0)
    lane = lax.broadcasted_iota(jnp.int32, (GRID_W, LANES), 1)
    low_half = lane < GRID_W
    kc = jnp.where(low_half, lane, lane - GRID_W)
    cstart = jnp.clip(qc - NA_WIN_C // 2, 0, GRID_W - NA_WIN_C)
    valid = (kc >= cstart) & (kc < cstart + NA_WIN_C)
    for h in range(rpb_ref.shape[0]):
        for i in range(NA_BIAS_SLOTS):
            row_l = jnp.broadcast_to(rpb_ref[h, pl.ds(max(i - 1, 0), 1), :], (GRID_W, LANES))
            row_r = jnp.broadcast_to(rpb_ref[h, pl.ds(min(i, NA_N_DR - 1), 1), :], (GRID_W, LANES))
            left = pltpu.roll(row_l, LANES - (NA_WIN_C - 1), 1, stride=1, stride_axis=0)
            right = pltpu.roll(row_r, GRID_W - (NA_WIN_C - 1), 1, stride=1, stride_axis=0)
            tile = jnp.where(low_half, left, right)
            o_ref[h, i] = jnp.where(valid, tile * LOG2E, NEG_INF)


def _na_bias_table(rpb):
    n_dr, n_dc = rpb.shape[1:]
    rpb_rows = jnp.pad(rpb, ((0, 0), (0, 16 - n_dr), (0, LANES - n_dc)))
    heads_per_step = 4
    return pl.pallas_call(
        _na_bias_kernel,
        grid=(NA_HEADS // heads_per_step,),
        in_specs=[pl.BlockSpec((heads_per_step, 16, LANES), lambda h: (h, 0, 0))],
        out_specs=pl.BlockSpec((heads_per_step, NA_BIAS_SLOTS, GRID_W, LANES), lambda h: (h, 0, 0, 0)),
        out_shape=jax.ShapeDtypeStruct((NA_HEADS, NA_BIAS_SLOTS, GRID_W, LANES), F32),
        compiler_params=_params(1),
        name="na_bias_table",
    )(rpb_rows)


NA_ROWS_PER_GROUP = 4
NA_BAND_ROWS = NA_ROWS_PER_GROUP + NA_WIN_R


def _na_lat_attn_kernel(n_rows, q_ref, k_ref, v_ref, kc_ref, vc_ref, b_ref, z_ref, o_ref):
    n_q = NA_ROWS_PER_GROUP * GRID_W
    low_half = lax.broadcasted_iota(jnp.int32, (1, LANES), 1) < NA_HEAD_DIM
    mask_left = jnp.where(low_half, NEG_INF, 0.0)
    mask_right = jnp.where(low_half, 0.0, NEG_INF)
    no_weight = jnp.zeros((GRID_W, LANES), BF16)
    k_ctx, v_ctx = kc_ref[0], vc_ref[0]
    for grp in range(n_rows // NA_ROWS_PER_GROUP):
        r0 = grp * NA_ROWS_PER_GROUP
        band0 = min(max(r0 - NA_WIN_R // 2, 0), n_rows - NA_BAND_ROWS)
        rows = slice(grp * n_q, (grp + 1) * n_q)
        keys = slice(band0 * GRID_W, (band0 + NA_BAND_ROWS) * GRID_W)
        qs = _split_head_pair(q_ref[0, rows, :], low_half)
        s_loc = _dot_nt(qs, k_ref[0, keys, :])
        s_ctx = _dot_nt(qs, k_ctx)
        p_loc, p_ctx, denoms = [], [], []
        for e in range(2):
            for rr in range(NA_ROWS_PER_GROUP):
                r = r0 + rr
                win0 = min(max(r - NA_WIN_R // 2, 0), n_rows - NA_WIN_R)
                qrows = slice(e * n_q + rr * GRID_W, e * n_q + (rr + 1) * GRID_W)
                blocks = {}
                for cblk in range(NA_BAND_ROWS // 2):
                    key_row = band0 + 2 * cblk
                    left_ok = win0 <= key_row < win0 + NA_WIN_R
                    right_ok = win0 <= key_row + 1 < win0 + NA_WIN_R
                    if not (left_ok or right_ok):
                        continue
                    t = s_loc[qrows, cblk * LANES:(cblk + 1) * LANES] + b_ref[e, key_row - r + NA_WIN_R]
                    if not left_ok:
                        t = t + mask_left
                    if not right_ok:
                        t = t + mask_right
                    blocks[cblk] = t
                ctx = [s_ctx[qrows, c0:c0 + LANES] for c0 in range(0, s_ctx.shape[1], LANES)]
                pieces = list(blocks.values()) + ctx
                m = jnp.max(functools.reduce(jnp.maximum, pieces), axis=1, keepdims=True)
                blocks = {c: jnp.exp2(t - m) for c, t in blocks.items()}
                ctx = [jnp.exp2(t - m) for t in ctx]
                denoms.append(jnp.sum(functools.reduce(jnp.add, list(blocks.values()) + ctx), axis=1, keepdims=True))
                p_loc.append(jnp.concatenate(
                    [blocks[c].astype(BF16) if c in blocks else no_weight for c in range(NA_BAND_ROWS // 2)], axis=1))
                p_ctx.append(jnp.concatenate([t.astype(BF16) for t in ctx], axis=1))
        o = _dot(jnp.concatenate(p_loc, axis=0), v_ref[0, keys, :]) + _dot(jnp.concatenate(p_ctx, axis=0), v_ctx)
        o = o / jnp.concatenate(denoms, axis=0)
        o2 = jnp.where(low_half, o[:n_q], o[n_q:])
        o_ref[0, rows, :] = (o2 * _silu(z_ref[0, rows, :].astype(F32))).astype(BF16)


def _na_lat_attention(q, k, v, k_ctx, v_ctx, bias, z):
    bsz, seq_len, _ = q.shape
    n_rows = seq_len // GRID_W
    past = k_ctx.shape[1]
    assert n_rows % NA_ROWS_PER_GROUP == 0 and n_rows >= NA_BAND_ROWS
    seq_blk = pl.BlockSpec((1, seq_len, LANES), lambda b, j: (b, 0, j))
    ctx_blk = pl.BlockSpec((1, past, LANES), lambda b, j: (b, 0, j))
    return pl.pallas_call(
        functools.partial(_na_lat_attn_kernel, n_rows),
        grid=(bsz, NA_PAIRS),
        in_specs=[seq_blk, seq_blk, seq_blk, ctx_blk, ctx_blk,
                  pl.BlockSpec((2, NA_BIAS_SLOTS, GRID_W, LANES), lambda b, j: (j, 0, 0, 0)),
                  seq_blk],
        out_specs=seq_blk,
        out_shape=jax.ShapeDtypeStruct((bsz, seq_len, NA_WIDTH), BF16),
        compiler_params=_params(2),
        name="na_lat_attention",
    )(q, k, v, k_ctx, v_ctx, bias, z)


def _rope_swap_perm():
    nf = QK_ROPE // 4
    idx = np.arange(QK_ROPE).reshape(2, 2, nf)
    return idx[:, ::-1, :].reshape(-1)


def _rope_table(seq_len):
    nf = QK_ROPE // 4
    inv = ROPE_BASE ** (-jnp.arange(nf, dtype=F32) / nf)
    t = jnp.arange(seq_len)
    pos = jnp.stack([t // GRID_W, t % GRID_W], -1).astype(F32)
    ang = pos[:, :, None] * inv
    cos = jnp.cos(ang)
    sin = jnp.sin(ang)
    c = jnp.concatenate([cos, cos], axis=-1).reshape(seq_len, QK_ROPE)
    s = jnp.concatenate([-sin, sin], axis=-1).reshape(seq_len, QK_ROPE)
    return jnp.concatenate([c, s], axis=-1)


def _mla_weights(w_in, w_uq):
    perm = _rope_swap_perm()
    o1, o2, o3 = Q_LORA, Q_LORA + KV_LORA, Q_LORA + KV_LORA + QK_ROPE
    w_kr = w_in[:, o2:o3]
    w1 = jnp.concatenate([w_in[:, :o2], w_kr, w_kr[:, perm], w_in[:, o3:]], axis=1).astype(BF16)
    wq = w_uq.reshape(Q_LORA, MLA_HEADS, QK_NOPE + QK_ROPE)
    nope = wq[:, :, :QK_NOPE].reshape(Q_LORA, MLA_HEADS * QK_NOPE)
    rope = wq[:, :, QK_NOPE:]
    rope_blk = jnp.concatenate([rope, rope[:, :, perm]], axis=-1).reshape(Q_LORA, MLA_HEADS * LANES)
    return w1, jnp.concatenate([nope, rope_blk], axis=1).astype(BF16)


def kernel(x_prompt, x_sample, cache_mla_ckv, cache_mla_krope, cache_na_k, cache_na_v, c, c_ctx, ada_w, ada_b, ln_g, ln_b, pool_w_in, pool_w_grp, pool_scale, pool_w_out, mla_w_in, mla_q_norm, mla_w_uq, mla_kv_norm, mla_w_ukv, mla_w_out, na_w_in, na_rpb, na_w_out):
    n_ctx, n_lat = x_prompt.shape[0], x_sample.shape[0]
    lat_len = x_sample.shape[1]
    cond = jnp.concatenate([c_ctx[None], c, jnp.zeros((8 - 1 - n_lat, D_MODEL), F32)], axis=0)
    mod = _ada(cond, ada_w, ada_b)
    yp, ys = x_prompt, x_sample
    st_ckv = st_kr = st_k = st_v = None
    ctx_len = x_prompt.shape[1]
    ctx_fold = 2 if (n_ctx % 2 == 0 and 2 * ctx_len <= ROW_TILE) else 1
    fold = lambda t: t.reshape((n_ctx // ctx_fold, ctx_fold * ctx_len) + t.shape[2:])
    unfold = lambda t: t.reshape((n_ctx, ctx_len) + t.shape[2:])
    pending = None
    mod_ctx_rows = mod[:, 0:1]
    mod_lat_rows = mod[:, 1:1 + n_lat].reshape(DEPTH * n_lat, 1, 3 * D_MODEL)
    ln_g_rows, ln_b_rows = ln_g[:, None, :], ln_b[:, None, :]
    pool_scale_rows = pool_scale[:, None, :]
    for i in range(DEPTH):
        kind, j = i % N_MIXERS, i // N_MIXERS
        mod_p = mod_pf = (mod_ctx_rows, i, False)
        mod_s = (mod_lat_rows, i * n_lat, True)
        g, b = (ln_g_rows, i), (ln_b_rows, i)
        if kind == 1 and pending is not None:
            ys = _out_layer(pending[0], ys, *pending[1:])
            pending = None
        if kind == 0:
            ps = (pool_scale_rows, j)
            yp = unfold(_pool_layer(fold(yp), ctx_len, mod_pf, j, pool_w_in, pool_w_grp, ps, pool_w_out, g, b))
            ys = _pool_layer(ys, lat_len, mod_s, j, pool_w_in, pool_w_grp, ps, pool_w_out, g, b, prev=pending)
            pending = None
        elif kind == 1:
            w1, w_uq = _mla_weights(mla_w_in[j], mla_w_uq[j])
            w_ukv = mla_w_ukv[j].astype(BF16)
            qn, kvn = mla_q_norm[j][None], mla_kv_norm[j][None]
            q_p, kv_p, krp_p, z_p, st_ckv, st_kr = _mla_proj(fold(yp), mod_pf, w1, qn, kvn, w_uq, w_ukv, None, ctx_fold)
            yp = _mla_ctx_layer(unfold(q_p), unfold(z_p), unfold(kv_p), unfold(krp_p), yp, mod_p, j, mla_w_out, g, b)
            q_s, kv_s, krp_s, z_s = _mla_proj(ys, mod_s, w1, qn, kvn, w_uq, w_ukv, _rope_table(lat_len), False)
            kv_c, krp_c = _mla_cache_keys(cache_mla_ckv, cache_mla_krope, j, w_ukv)
            a_s = _mla_attention(q_s, z_s, [(kv_c, krp_c), (kv_s, krp_s)], lat_len, 2, 2 * ATTN_ROWS)
            pending = (a_s, mod_s, j, mla_w_out, g, b)
        else:
            q_p, k_p, v_p, z_p, st_k, st_v = _na_proj(fold(yp), mod_pf, j, na_w_in, ctx_fold)
            yp = _na_ctx_layer(unfold(q_p), unfold(k_p), unfold(v_p), unfold(z_p), yp, mod_p, j, na_w_out, g, b)
            if pending is None:
                q_s, k_s, v_s, z_s = _na_proj(ys, mod_s, j, na_w_in, False)
            else:
                q_s, k_s, v_s, z_s, ys = _na_proj(ys, mod_s, j, na_w_in, False, prev=pending)
                pending = None
            past = cache_na_k.shape[2]
            k_c = cache_na_k[:, j].reshape(n_lat, past, NA_WIDTH).astype(BF16)
            v_c = cache_na_v[:, j].reshape(n_lat, past, NA_WIDTH).astype(BF16)
            bias = _na_bias_table(na_rpb[j])
            a_s = _na_lat_attention(q_s, k_s, v_s, k_c, v_c, bias, z_s)
            pending = (a_s, mod_s, j, na_w_out, g, b)
    if pending is not None:
        ys = _out_layer(pending[0], ys, *pending[1:])
    state_na_k = st_k.reshape(n_ctx, 1, -1, NA_HEADS, NA_HEAD_DIM)
    state_na_v = st_v.reshape(n_ctx, 1, -1, NA_HEADS, NA_HEAD_DIM)
    return (yp, ys, st_ckv, st_kr, state_na_k, state_na_v)
```
